```python
import math
import jax, jax.numpy as jnp
from jax import lax
import numpy as np

D_MODEL = 1024
BATCH = 8
SEQ = 8192
DEPTH = 2

GRID_W = 64
CTX_LEN = 256
N_BRANCH = 3
BRANCH_W = 1024
Q_BLOCK = 128
EPS = 1e-6
ROPE_BASE = 10000.0
NEG_INF = -1e30

MLA_HEADS = 8
MLA_NOPE = 64
MLA_ROPE = 32
MLA_V = 128
MLA_Q_RANK = 384
MLA_KV_RANK = 256

DIFF_HEADS = 8
DIFF_HD = 64
DIFF_V = 2 * DIFF_HD

SWA_HEADS = 16
SWA_KV_HEADS = 4
SWA_GROUP = SWA_HEADS // SWA_KV_HEADS
SWA_HD = 64
WINDOW = 128

IN_SPLITS = (MLA_Q_RANK, MLA_KV_RANK, MLA_ROPE,
             DIFF_HEADS * 2 * DIFF_HD, DIFF_HEADS * 2 * DIFF_HD, DIFF_HEADS * DIFF_V,
             SWA_HEADS * SWA_HD, SWA_KV_HEADS * SWA_HD, SWA_KV_HEADS * SWA_HD,
             N_BRANCH * BRANCH_W, N_BRANCH * D_MODEL)
D_IN = sum(IN_SPLITS)

kernel_name = 'hybrid_mla_diff_swa_dit'


def rms_norm(x, g):
    xf = x.astype(jnp.float32)
    y = xf * lax.rsqrt(jnp.mean(xf * xf, axis=-1, keepdims=True) + EPS)
    return (y * g.astype(jnp.float32)).astype(x.dtype)


def split_cols(p):
    idx = np.cumsum(np.array(IN_SPLITS))[:-1].tolist()
    return jnp.split(p, idx, axis=-1)


def axial_rope_tables(row, col, rot_dim):
    n_freq = rot_dim // 4
    inv_freq = ROPE_BASE ** (-jnp.arange(n_freq, dtype=jnp.float32) / n_freq)
    ang = jnp.concatenate([row.astype(jnp.float32)[:, None] * inv_freq,
                           col.astype(jnp.float32)[:, None] * inv_freq], axis=-1)
    return jnp.cos(ang), jnp.sin(ang)


def apply_rope(x, cos, sin):
    half = x.shape[-1] // 2
    x1, x2 = x[..., :half], x[..., half:]
    cs = cos[:, None, :].astype(x.dtype)
    sn = sin[:, None, :].astype(x.dtype)
    return jnp.concatenate([x1 * cs - x2 * sn, x2 * cs + x1 * sn], axis=-1)


def sweep_query_blocks(fn, *qs):
    b, n = qs[0].shape[:2]
    nb = n // Q_BLOCK
    blocked = tuple(jnp.moveaxis(q.reshape(b, nb, Q_BLOCK, *q.shape[2:]), 1, 0) for q in qs)
    out = lax.map(lambda args: fn(*args), (jnp.arange(nb),) + blocked)
    out = jnp.moveaxis(out, 0, 1)
    return out.reshape(b, n, *out.shape[3:])


def dense_attend(q, k, v):
    s = jnp.einsum('bqhd,bkhd->bhqk', q * (q.shape[-1] ** -0.5), k)
    p = jax.nn.softmax(s.astype(jnp.float32), axis=-1).astype(v.dtype)
    return jnp.einsum('bhqk,bkhd->bqhd', p, v)


def diff_attend(q, k, v, lam):
    s = jnp.einsum('bqhmd,bkhmd->bhmqk', q * (DIFF_HD ** -0.5), k)
    p = jax.nn.softmax(s.astype(jnp.float32), axis=-1)
    a = (p[:, :, 0] - lam * p[:, :, 1]).astype(v.dtype)
    return jnp.einsum('bhqk,bkhd->bqhd', a, v)


def diff_post(o, subln, lam_init):
    b, n = o.shape[:2]
    return (rms_norm(o, subln) * (1.0 - lam_init)).reshape(b, n, DIFF_HEADS * DIFF_V)


def sink_softmax(sink, parts):
    b, h, g, q = parts[0].shape[:4]
    sk = jnp.broadcast_to(sink.astype(jnp.float32).reshape(1, h, g, 1, 1), (b, h, g, q, 1))
    p = jax.nn.softmax(jnp.concatenate([sk] + [s.astype(jnp.float32) for s in parts], axis=-1), axis=-1)
    out, off = [], 1
    for s in parts:
        out.append(p[..., off:off + s.shape[-1]])
        off += s.shape[-1]
    return out


def mla_heads(qc, kvc, kr, q_norm, w_uq, kv_norm, w_ukv, rope):
    b, n = qc.shape[:2]
    q = (rms_norm(qc, q_norm) @ w_uq).reshape(b, n, MLA_HEADS, MLA_NOPE + MLA_ROPE)
    kv = (rms_norm(kvc, kv_norm) @ w_ukv).reshape(b, n, MLA_HEADS, MLA_NOPE + MLA_V)
    q_nope, q_rot = q[..., :MLA_NOPE], q[..., MLA_NOPE:]
    k_nope, v = kv[..., :MLA_NOPE], kv[..., MLA_NOPE:]
    k_rot = kr[:, :, None, :]
    if rope is not None:
        q_rot = apply_rope(q_rot, *rope)
        k_rot = apply_rope(k_rot, *rope)
    k_rot = jnp.broadcast_to(k_rot, (b, n, MLA_HEADS, MLA_ROPE))
    return (jnp.concatenate([q_nope, q_rot], axis=-1), jnp.concatenate([k_nope, k_rot], axis=-1), v)


def diff_heads(q, k, v, rope):
    b, n = q.shape[:2]
    q = q.reshape(b, n, 2 * DIFF_HEADS, DIFF_HD)
    k = k.reshape(b, n, 2 * DIFF_HEADS, DIFF_HD)
    if rope is not None:
        q = apply_rope(q, *rope)
        k = apply_rope(k, *rope)
    return (q.reshape(b, n, DIFF_HEADS, 2, DIFF_HD), k.reshape(b, n, DIFF_HEADS, 2, DIFF_HD),
            v.reshape(b, n, DIFF_HEADS, DIFF_V))


def swa_heads(q, k, v, rope):
    b, n = q.shape[:2]
    q = q.reshape(b, n, SWA_HEADS, SWA_HD)
    k = k.reshape(b, n, SWA_KV_HEADS, SWA_HD)
    v = v.reshape(b, n, SWA_KV_HEADS, SWA_HD)
    if rope is not None:
        q = apply_rope(q, *rope)
        k = apply_rope(k, *rope)
    return q, k, v


def swa_latent(q, k, v, k_ctx, v_ctx, sink):
    b, n_seq = q.shape[:2]
    span = Q_BLOCK + 2 * WINDOW
    pad = ((0, 0), (WINDOW, WINDOW), (0, 0), (0, 0))
    kp, vp = jnp.pad(k, pad), jnp.pad(v, pad)

    def block(i, qb):
        start = i * Q_BLOCK
        kb = lax.dynamic_slice_in_dim(kp, start, span, axis=1)
        vb = lax.dynamic_slice_in_dim(vp, start, span, axis=1)
        qpos = start + jnp.arange(Q_BLOCK)
        kpos = start - WINDOW + jnp.arange(span)
        mask = (jnp.abs(qpos[:, None] - kpos[None, :]) <= WINDOW) & (kpos >= 0)[None, :] & (kpos < n_seq)[None, :]
        qg = qb.reshape(b, Q_BLOCK, SWA_KV_HEADS, SWA_GROUP, SWA_HD) * (SWA_HD ** -0.5)
        s_loc = jnp.where(mask, jnp.einsum('bqhgd,bkhd->bhgqk', qg, kb).astype(jnp.float32), NEG_INF)
        s_ctx = jnp.einsum('bqhgd,bkhd->bhgqk', qg, k_ctx)
        p_ctx, p_loc = sink_softmax(sink, (s_ctx, s_loc))
        o = (jnp.einsum('bhgqk,bkhd->bqhgd', p_ctx.astype(v.dtype), v_ctx)
             + jnp.einsum('bhgqk,bkhd->bqhgd', p_loc.astype(v.dtype), vb))
        return o.reshape(b, Q_BLOCK, SWA_HEADS * SWA_HD)

    return sweep_query_blocks(block, q)


def swa_context(q, k, v, sink):
    b, n = q.shape[:2]
    qg = q.reshape(b, n, SWA_KV_HEADS, SWA_GROUP, SWA_HD) * (SWA_HD ** -0.5)
    (p,) = sink_softmax(sink, (jnp.einsum('bqhgd,bkhd->bhgqk', qg, k),))
    o = jnp.einsum('bhgqk,bkhd->bqhgd', p.astype(v.dtype), v)
    return o.reshape(b, n, SWA_HEADS * SWA_HD)


def merge_branches(ys, z, gm, w_branch, w_out):
    terms = []
    for r, y in enumerate(ys):
        zr = z[..., r * BRANCH_W:(r + 1) * BRANCH_W]
        gr = gm[..., r * D_MODEL:(r + 1) * D_MODEL]
        terms.append(jax.nn.sigmoid(gr) * ((y * jax.nn.silu(zr)) @ w_branch[r]))
    return (terms[0] + terms[1] + terms[2]) @ w_out


def hybrid_layer(x, ctx, mod_lat, mod_ctx, rope_mla, rope_hd, norm_g, w_in,
                 mla_q_norm, mla_w_uq, mla_kv_norm, mla_w_ukv,
                 lam, lam_init, diff_subln, swa_sink, w_branch, w_out, need_ctx):
    shift, scale, gate = jnp.split(mod_lat[:, None, :], 3, axis=-1)
    shift_c, scale_c, gate_c = jnp.split(mod_ctx, 3, axis=-1)
    h_lat = rms_norm(x, norm_g) * (1.0 + scale) + shift
    h_ctx = rms_norm(ctx, norm_g) * (1.0 + scale_c) + shift_c
    qc_l, kvc_l, kr_l, dq_l, dk_l, dv_l, sq_l, sk_l, sv_l, z_l, gm_l = split_cols(h_lat @ w_in)
    qc_c, kvc_c, kr_c, dq_c, dk_c, dv_c, sq_c, sk_c, sv_c, z_c, gm_c = split_cols(h_ctx @ w_in)

    mq_l, mk_l, mv_l = mla_heads(qc_l, kvc_l, kr_l, mla_q_norm, mla_w_uq, mla_kv_norm, mla_w_ukv, rope_mla)
    mq_c, mk_c, mv_c = mla_heads(qc_c, kvc_c, kr_c, mla_q_norm, mla_w_uq, mla_kv_norm, mla_w_ukv, None)
    mk_all = jnp.concatenate([mk_c, mk_l], axis=1)
    mv_all = jnp.concatenate([mv_c, mv_l], axis=1)
    b, n = x.shape[:2]
    ya_l = sweep_query_blocks(lambda i, qb: dense_attend(qb, mk_all, mv_all), mq_l).reshape(b, n, BRANCH_W)

    dq_l, dk_l, dv_l = diff_heads(dq_l, dk_l, dv_l, rope_hd)
    dq_c, dk_c, dv_c = diff_heads(dq_c, dk_c, dv_c, None)
    dk_all = jnp.concatenate([dk_c, dk_l], axis=1)
    dv_all = jnp.concatenate([dv_c, dv_l], axis=1)
    yb_l = diff_post(sweep_query_blocks(lambda i, qb: diff_attend(qb, dk_all, dv_all, lam), dq_l),
                     diff_subln, lam_init)

    sq_l, sk_l, sv_l = swa_heads(sq_l, sk_l, sv_l, rope_hd)
    sq_c, sk_c, sv_c = swa_heads(sq_c, sk_c, sv_c, None)
    yc_l = swa_latent(sq_l, sk_l, sv_l, sk_c, sv_c, swa_sink)

    x = x + gate * merge_branches((ya_l, yb_l, yc_l), z_l, gm_l, w_branch, w_out)

    if need_ctx:
        bc, nc = ctx.shape[:2]
        ya_c = dense_attend(mq_c, mk_c, mv_c).reshape(bc, nc, BRANCH_W)
        yb_c = diff_post(diff_attend(dq_c, dk_c, dv_c, lam), diff_subln, lam_init)
        yc_c = swa_context(sq_c, sk_c, sv_c, swa_sink)
        ctx = ctx + gate_c * merge_branches((ya_c, yb_c, yc_c), z_c, gm_c, w_branch, w_out)
    return x, ctx


def setup_inputs(seed: int = 0) -> dict:
    key = jax.random.key(seed)
    ks = jax.random.split(key, 22)
    f32 = jnp.float32

    def nrm(k, shape, s):
        return jax.random.normal(k, shape, f32) * s

    L = DEPTH
    return {
        'x': nrm(ks[0], (BATCH, SEQ, D_MODEL), 1.0),
        'c': nrm(ks[1], (BATCH, D_MODEL), 1.0),
        'ctx': nrm(ks[2], (BATCH, CTX_LEN, D_MODEL), 1.0),
        'c_ctx': nrm(ks[3], (D_MODEL,), 1.0),
        'w_mod': nrm(ks[4], (L, D_MODEL, 3 * D_MODEL), D_MODEL ** -0.5),
        'b_mod': nrm(ks[5], (L, 3 * D_MODEL), 0.01),
        'norm_g': 1.0 + nrm(ks[6], (L, D_MODEL), 0.02),
        'w_in': nrm(ks[7], (L, D_MODEL, D_IN), D_MODEL ** -0.5),
        'mla_q_norm': 1.0 + nrm(ks[8], (L, MLA_Q_RANK), 0.02),
        'mla_w_uq': nrm(ks[9], (L, MLA_Q_RANK, MLA_HEADS * (MLA_NOPE + MLA_ROPE)), MLA_Q_RANK ** -0.5),
        'mla_kv_norm': 1.0 + nrm(ks[10], (L, MLA_KV_RANK), 0.02),
        'mla_w_ukv': nrm(ks[11], (L, MLA_KV_RANK, MLA_HEADS * (MLA_NOPE + MLA_V)), MLA_KV_RANK ** -0.5),
        'diff_lq1': nrm(ks[12], (L, DIFF_HD), 0.1),
        'diff_lk1': nrm(ks[13], (L, DIFF_HD), 0.1),
        'diff_lq2': nrm(ks[14], (L, DIFF_HD), 0.1),
        'diff_lk2': nrm(ks[15], (L, DIFF_HD), 0.1),
        'diff_subln': 1.0 + nrm(ks[16], (L, DIFF_V), 0.02),
        'swa_sink': nrm(ks[17], (L, SWA_HEADS), 0.5),
        'w_branch': nrm(ks[18], (L, N_BRANCH, BRANCH_W, D_MODEL), BRANCH_W ** -0.5),
        'w_out': nrm(ks[19], (L, D_MODEL, D_MODEL), D_MODEL ** -0.5),
        'final_norm': 1.0 + nrm(ks[20], (D_MODEL,), 0.02),
    }


def reference(x, c, ctx, c_ctx, w_mod, b_mod, norm_g, w_in, mla_q_norm, mla_w_uq, mla_kv_norm, mla_w_ukv,
              diff_lq1, diff_lk1, diff_lq2, diff_lk2, diff_subln, swa_sink, w_branch, w_out, final_norm):
    n_lat = x.shape[1]
    ROWS = n_lat // GRID_W
    row = jnp.repeat(jnp.arange(ROWS), GRID_W)
    col = jnp.tile(jnp.arange(GRID_W), ROWS)
    rope_mla = axial_rope_tables(row, col, MLA_ROPE)
    rope_hd = axial_rope_tables(row, col, SWA_HD)
    silu_c = jax.nn.silu(c)
    silu_cc = jax.nn.silu(c_ctx)
    f32 = jnp.float32
    for l in range(DEPTH):
        mod_lat = silu_c @ w_mod[l] + b_mod[l]
        mod_ctx = silu_cc @ w_mod[l] + b_mod[l]
        lam_init = 0.8 - 0.6 * math.exp(-0.3 * l)
        lam = (jnp.exp(jnp.sum(diff_lq1[l].astype(f32) * diff_lk1[l].astype(f32)))
               - jnp.exp(jnp.sum(diff_lq2[l].astype(f32) * diff_lk2[l].astype(f32))) + lam_init)
        x, ctx = hybrid_layer(x, ctx, mod_lat, mod_ctx, rope_mla, rope_hd, norm_g[l], w_in[l],
                              mla_q_norm[l], mla_w_uq[l], mla_kv_norm[l], mla_w_ukv[l],
                              lam, lam_init, diff_subln[l], swa_sink[l], w_branch[l], w_out[l],
                              l < DEPTH - 1)
    return rms_norm(x, final_norm)
```

```python
import functools
import math

import jax
import jax.numpy as jnp
from jax import lax
from jax.experimental import pallas as pl
from jax.experimental.pallas import tpu as pltpu

F32 = jnp.float32
BF16 = jnp.bfloat16

GRID_W = 64
N_BRANCH = 3
EPS = 1e-6
ROPE_BASE = 10000.0
NEG_INF = -1e30
LOG2E = math.log2(math.e)

MLA_HEADS = 8
MLA_NOPE = 64
MLA_ROPE = 32
MLA_V = 128
MLA_Q_RANK = 384
MLA_KV_RANK = 256

DIFF_HEADS = 8
DIFF_HD = 64
DIFF_V = 2 * DIFF_HD

SWA_HEADS = 16
SWA_KV_HEADS = 4
SWA_GROUP = SWA_HEADS // SWA_KV_HEADS
SWA_HD = 64
WINDOW = 128

LANE = 128

VMEM_LIMIT = 56 * 1024 * 1024

PROJ_TM = 256
MERGE_TM = 256
FLASH_TQ = 512
FLASH_TK = 512
SWA_TQ = 256


def _const_spec(shape):
    zeros = (0,) * len(shape)
    return pl.BlockSpec(shape, lambda *_: zeros, pipeline_mode=pl.Buffered(1))


def _rms(x, eps=EPS):
    return x * lax.rsqrt(jnp.mean(x * x, axis=-1, keepdims=True) + eps)


def _mod_kernel(c_ref, w_ref, b_ref, o_ref):
    cv = c_ref[...]
    s = cv * jax.nn.sigmoid(cv)
    o_ref[0] = jnp.dot(s, w_ref[0], preferred_element_type=F32) + b_ref[0]


def _modulation(c_rows, w_mod, b_mod):
    depth, d, d3 = w_mod.shape
    rows = c_rows.shape[0]
    tn = d
    return pl.pallas_call(
        _mod_kernel,
        grid=(depth, d3 // tn),
        in_specs=[
            pl.BlockSpec((rows, d), lambda l, n: (0, 0)),
            pl.BlockSpec((1, d, tn), lambda l, n: (l, 0, n)),
            pl.BlockSpec((1, 1, tn), lambda l, n: (l, 0, n)),
        ],
        out_specs=pl.BlockSpec((1, rows, tn), lambda l, n: (l, 0, n)),
        out_shape=jax.ShapeDtypeStruct((depth, rows, d3), F32),
        compiler_params=pltpu.CompilerParams(
            dimension_semantics=("arbitrary", "arbitrary"), vmem_limit_bytes=VMEM_LIMIT),
    )(c_rows, w_mod, b_mod.reshape(depth, 1, d3))


def _rope(x, cc, ss, lo_mask, half):
    partner = jnp.where(lo_mask, pltpu.roll(x, LANE - half, axis=1), pltpu.roll(x, half, axis=1))
    return x * cc + partner * ss


def _proj_kernel(x_ref, mod_ref, ng_ref, w1_ref, qn_ref, kvn_ref, wuq_ref, wukv_ref, wd_ref, ws_ref,
                 cch_ref, ssh_ref, ccm_ref, ssm_ref,
                 mqt_ref, mk_ref, mvt_ref, dqt_ref, dk_ref, dvt_ref, sqt_ref, sk_ref, svt_ref,
                 *, mla_scale, hd_scale):
    x = x_ref[0]
    shift = mod_ref[0, 0, 0:1, :]
    scale = mod_ref[0, 0, 1:2, :]
    h = (_rms(x) * ng_ref[...] * (1.0 + scale) + shift).astype(BF16)
    tm = x.shape[0]

    lane = lax.broadcasted_iota(jnp.int32, (tm, LANE), 1)
    lo_hd = (lane & (SWA_HD // 2)) == 0
    lo_mla = (lane & (MLA_ROPE // 2)) == 0
    cch, ssh = cch_ref[...], ssh_ref[...]
    ccm, ssm = ccm_ref[...], ssm_ref[...]

    t1 = jnp.dot(h, w1_ref[...], preferred_element_type=F32)
    qc = t1[:, :MLA_Q_RANK]
    kvc = t1[:, MLA_Q_RANK:MLA_Q_RANK + MLA_KV_RANK]
    kr = t1[:, MLA_Q_RANK + MLA_KV_RANK:]
    qn = (_rms(qc) * qn_ref[...]).astype(BF16)
    kvn = (_rms(kvc) * kvn_ref[...]).astype(BF16)
    q_m = jnp.dot(qn, wuq_ref[...], preferred_element_type=F32)
    kv_m = jnp.dot(kvn, wukv_ref[...], preferred_element_type=F32)
    kr_roped = _rope(kr, ccm, ssm, lo_mla, MLA_ROPE // 2)
    nh = MLA_HEADS * LANE
    for hh in range(MLA_HEADS):
        sl = slice(hh * LANE, (hh + 1) * LANE)
        qh = _rope(q_m[:, sl], ccm, ssm, lo_mla, MLA_ROPE // 2) * mla_scale
        mqt_ref[0, hh] = qh.T.astype(BF16)
        mk_ref[0, hh] = (kv_m[:, sl] + kr_roped).astype(BF16)
        mvt_ref[0, hh] = kv_m[:, nh + hh * LANE:nh + (hh + 1) * LANE].T.astype(BF16)

    t2 = jnp.dot(h, wd_ref[...], preferred_element_type=F32)
    nd = DIFF_HEADS * LANE
    for hh in range(DIFF_HEADS):
        sl = slice(hh * LANE, (hh + 1) * LANE)
        qh = _rope(t2[:, sl], cch, ssh, lo_hd, DIFF_HD // 2) * hd_scale
        dqt_ref[0, hh] = qh.T.astype(BF16)
        kh = _rope(t2[:, nd + hh * LANE:nd + (hh + 1) * LANE], cch, ssh, lo_hd, DIFF_HD // 2)
        dk_ref[0, hh] = kh.astype(BF16)
        dvt_ref[0, hh] = t2[:, 2 * nd + hh * LANE:2 * nd + (hh + 1) * LANE].T.astype(BF16)

    t3 = jnp.dot(h, ws_ref[...], preferred_element_type=F32)
    nq = SWA_HEADS * SWA_HD
    nkv = SWA_KV_HEADS * SWA_HD
    for j in range(nq // LANE):
        qt = (_rope(t3[:, j * LANE:(j + 1) * LANE], cch, ssh, lo_hd, SWA_HD // 2) * hd_scale).T.astype(BF16)
        sqt_ref[0, 2 * j] = qt[:SWA_HD]
        sqt_ref[0, 2 * j + 1] = qt[SWA_HD:]
    for j in range(nkv // LANE):
        kh = _rope(t3[:, nq + j * LANE:nq + (j + 1) * LANE], cch, ssh, lo_hd, SWA_HD // 2).astype(BF16)
        sk_ref[0, 2 * j] = kh[:, :SWA_HD]
        sk_ref[0, 2 * j + 1] = kh[:, SWA_HD:]
        vt = t3[:, nq + nkv + j * LANE:nq + nkv + (j + 1) * LANE].T.astype(BF16)
        svt_ref[0, 2 * j] = vt[:SWA_HD]
        svt_ref[0, 2 * j + 1] = vt[SWA_HD:]


def _project(x_all, mod4, norm_g, wts, tabs, lat0):
    b, t, d = x_all.shape
    tm = PROJ_TM
    lat_blk = lat0 // tm
    mla_scale = (MLA_NOPE + MLA_ROPE) ** -0.5 * LOG2E
    hd_scale = SWA_HD ** -0.5 * LOG2E

    def headmajor(nh, natural, width=LANE):
        if natural:
            return (jax.ShapeDtypeStruct((b, nh, t, width), BF16),
                    pl.BlockSpec((1, nh, tm, width), lambda bi, i: (bi, 0, i, 0)))
        return (jax.ShapeDtypeStruct((b, nh, width, t), BF16),
                pl.BlockSpec((1, nh, width, tm), lambda bi, i: (bi, 0, 0, i)))

    outs = [headmajor(MLA_HEADS, False), headmajor(MLA_HEADS, True), headmajor(MLA_HEADS, False),
            headmajor(DIFF_HEADS, False), headmajor(DIFF_HEADS, True), headmajor(DIFF_HEADS, False),
            headmajor(SWA_HEADS, False, SWA_HD), headmajor(SWA_KV_HEADS, True, SWA_HD),
            headmajor(SWA_KV_HEADS, False, SWA_HD)]
    tab_spec = pl.BlockSpec((tm, LANE), lambda bi, i: (i, 0))
    in_specs = [
        pl.BlockSpec((1, tm, d), lambda bi, i: (bi, i, 0)),
        pl.BlockSpec((1, 1, 3, d), lambda bi, i: (bi, jnp.where(i >= lat_blk, 1, 0), 0, 0)),
        _const_spec((1, d)),
        _const_spec(wts["w1"].shape), _const_spec((1, MLA_Q_RANK)), _const_spec((1, MLA_KV_RANK)),
        _const_spec(wts["wuq"].shape), _const_spec(wts["wukv"].shape),
        _const_spec(wts["wd"].shape), _const_spec(wts["ws"].shape),
        tab_spec, tab_spec, tab_spec, tab_spec,
    ]
    return pl.pallas_call(
        functools.partial(_proj_kernel, mla_scale=mla_scale, hd_scale=hd_scale),
        grid=(b, t // tm),
        in_specs=in_specs,
        out_specs=[o[1] for o in outs],
        out_shape=[o[0] for o in outs],
        compiler_params=pltpu.CompilerParams(
            dimension_semantics=("parallel", "arbitrary"), vmem_limit_bytes=VMEM_LIMIT),
    )(x_all, mod4, norm_g, wts["w1"], wts["qn"], wts["kvn"], wts["wuq"], wts["wukv"], wts["wd"], wts["ws"],
      tabs["cch"], tabs["ssh"], tabs["ccm"], tabs["ssm"])


def _flash_kernel(*refs, n_sub, tq, tk, ctx_len, lat0, n_chunks, first_blk, lam_scale):
    if n_sub == 2:
        lam_ref, qt_ref, k_ref, vt_ref, subln_ref, o_ref, acc_ref = refs
    else:
        qt_ref, k_ref, vt_ref, o_ref, acc_ref = refs
    qt = qt_ref[0, 0]
    if n_sub == 2:
        row = lax.broadcasted_iota(jnp.int32, qt.shape, 0)
        zero = jnp.zeros_like(qt)
        qt = jnp.concatenate([jnp.where(row < DIFF_HD, qt, zero), jnp.where(row >= DIFF_HD, qt, zero)], axis=1)

    def scores(start, size):
        kc = k_ref[0, 0, pl.ds(start, size), :]
        return jnp.dot(kc, qt, preferred_element_type=F32)

    def pv(start, size, p):
        vc = vt_ref[0, 0, :, pl.ds(start, size)]
        return jnp.dot(vc, p.astype(BF16), preferred_element_type=F32)

    s = scores(0, ctx_len)
    m0 = jnp.max(s, axis=0, keepdims=True)
    p = jnp.exp2(s - m0)
    l0 = jnp.sum(p, axis=0, keepdims=True)
    acc_ref[...] = pv(0, ctx_len, p)

    def body(c, carry):
        m, l = carry
        start = pl.multiple_of(lat0 + c * tk, tk)
        s = scores(start, tk)
        m_new = jnp.maximum(m, jnp.max(s, axis=0, keepdims=True))
        p = jnp.exp2(s - m_new)
        alpha = jnp.exp2(m - m_new)
        l_new = alpha * l + jnp.sum(p, axis=0, keepdims=True)
        acc_ref[...] = alpha * acc_ref[...] + pv(start, tk, p)
        return m_new, l_new

    is_lat = (pl.program_id(2) + first_blk) > 0
    trips = jnp.where(is_lat, n_chunks, 0)
    m, l = lax.fori_loop(0, trips, body, (m0, l0))

    o = acc_ref[...] / l
    if n_sub == 2:
        lam = lam_ref[0]
        d = o[:, :tq] - lam * o[:, tq:]
        y = d * lax.rsqrt(jnp.mean(d * d, axis=0, keepdims=True) + EPS) * subln_ref[...] * lam_scale
    else:
        y = o
    o_ref[0] = y.T.astype(o_ref.dtype)


def _flash(qt, k, vt, *, n_sub, ctx_len, lat0, need_ctx, lam=None, subln=None, lam_scale=1.0):
    b, nh, _, t = qt.shape
    tq, tk = FLASH_TQ, FLASH_TK
    n_chunks = (t - lat0) // tk
    first_blk = 0 if need_ctx else lat0 // tq
    nq = t // tq - first_blk
    in_specs = [
        pl.BlockSpec((1, 1, LANE, tq), lambda bi, hi, i: (bi, hi, 0, i + first_blk)),
        pl.BlockSpec((1, 1, t, LANE), lambda bi, hi, i: (bi, hi, 0, 0)),
        pl.BlockSpec((1, 1, LANE, t), lambda bi, hi, i: (bi, hi, 0, 0)),
    ]
    args = [qt, k, vt]
    if n_sub == 2:
        in_specs = [pl.BlockSpec(memory_space=pltpu.SMEM)] + in_specs + [_const_spec((LANE, 1))]
        args = [lam] + args + [subln]
    return pl.pallas_call(
        functools.partial(_flash_kernel, n_sub=n_sub, tq=tq, tk=tk, ctx_len=ctx_len, lat0=lat0,
                          n_chunks=n_chunks, first_blk=first_blk, lam_scale=lam_scale),
        grid=(b, nh, nq),
        in_specs=in_specs,
        out_specs=pl.BlockSpec((1, tq, LANE), lambda bi, hi, i: (bi, i + first_blk, hi)),
        out_shape=jax.ShapeDtypeStruct((b, t, nh * LANE), BF16),
        scratch_shapes=[pltpu.VMEM((LANE, n_sub * tq), F32)],
        compiler_params=pltpu.CompilerParams(
            dimension_semantics=("parallel", "parallel", "arbitrary"), vmem_limit_bytes=VMEM_LIMIT),
    )(*args)


def _swa_kernel(sink_ref, qt_ref, k_ref, vt_ref, o_ref, *, tq, ctx_len, lat0, t_total, blk_of):
    g = pl.program_id(1)
    qs = blk_of(pl.program_id(2)) * tq
    span = tq + 2 * WINDOW
    ws = pl.multiple_of(jnp.clip(qs - WINDOW, 0, t_total - span), LANE)

    qt = jnp.concatenate([qt_ref[0, j] for j in range(SWA_GROUP)], axis=1)
    s_ctx = jnp.dot(k_ref[0, 0, 0:ctx_len, :], qt, preferred_element_type=F32)
    s_loc = jnp.dot(k_ref[0, 0, pl.ds(ws, span), :], qt, preferred_element_type=F32)

    kpos = ws - lat0 + lax.broadcasted_iota(jnp.int32, (span, tq), 0)
    qpos = qs - lat0 + lax.broadcasted_iota(jnp.int32, (span, tq), 1)
    mask = (jnp.abs(qpos - kpos) <= WINDOW) & (kpos >= 0)
    mask = jnp.concatenate([mask] * SWA_GROUP, axis=1)
    s_loc = jnp.where(mask, s_loc, NEG_INF)

    sink = jnp.concatenate(
        [jnp.full((1, tq), sink_ref[g * SWA_GROUP + j] * LOG2E, F32) for j in range(SWA_GROUP)], axis=1)
    m = jnp.maximum(jnp.maximum(jnp.max(s_ctx, axis=0, keepdims=True), jnp.max(s_loc, axis=0, keepdims=True)), sink)
    p_ctx = jnp.exp2(s_ctx - m)
    p_loc = jnp.exp2(s_loc - m)
    l = jnp.sum(p_ctx, axis=0, keepdims=True) + jnp.sum(p_loc, axis=0, keepdims=True) + jnp.exp2(sink - m)
    ot = (jnp.dot(vt_ref[0, 0, :, 0:ctx_len], p_ctx.astype(BF16), preferred_element_type=F32)
          + jnp.dot(vt_ref[0, 0, :, pl.ds(ws, span)], p_loc.astype(BF16), preferred_element_type=F32))
    ot = ot / l
    o_ref[0] = jnp.concatenate([ot[:, j * tq:(j + 1) * tq].T for j in range(SWA_GROUP)], axis=1).astype(o_ref.dtype)


def _swa(sink, qt, k, vt, *, ctx_len, lat0, need_ctx):
    b, _, _, t = qt.shape
    tq = SWA_TQ
    lat_blk = lat0 // tq
    if need_ctx:
        nq = (t - lat0 + ctx_len) // tq
        n_ctx_blk = ctx_len // tq
        blk_of = lambda i: jnp.where(i < n_ctx_blk, i, i + (lat_blk - n_ctx_blk))
    else:
        nq = (t - lat0) // tq
        blk_of = lambda i: i + lat_blk
    return pl.pallas_call(
        functools.partial(_swa_kernel, tq=tq, ctx_len=ctx_len, lat0=lat0, t_total=t, blk_of=blk_of),
        grid=(b, SWA_KV_HEADS, nq),
        in_specs=[
            pl.BlockSpec(memory_space=pltpu.SMEM),
            pl.BlockSpec((1, SWA_GROUP, SWA_HD, tq), lambda bi, gi, i: (bi, gi, 0, blk_of(i))),
            pl.BlockSpec((1, 1, t, SWA_HD), lambda bi, gi, i: (bi, gi, 0, 0)),
            pl.BlockSpec((1, 1, SWA_HD, t), lambda bi, gi, i: (bi, gi, 0, 0)),
        ],
        out_specs=pl.BlockSpec((1, tq, SWA_GROUP * SWA_HD), lambda bi, gi, i: (bi, blk_of(i), gi)),
        out_shape=jax.ShapeDtypeStruct((b, t, SWA_HEADS * SWA_HD), BF16),
        compiler_params=pltpu.CompilerParams(
            dimension_semantics=("parallel", "parallel", "arbitrary"), vmem_limit_bytes=VMEM_LIMIT),
    )(sink, qt, k, vt)


def _merge_kernel(*refs, final):
    if final:
        x_ref, mod_ref, ng_ref, ya_ref, yb_ref, yc_ref, wz_ref, wg_ref, wb_ref, wo_ref, fn_ref, o_ref = refs
    else:
        x_ref, mod_ref, ng_ref, ya_ref, yb_ref, yc_ref, wz_ref, wg_ref, wb_ref, wo_ref, o_ref = refs
    x = x_ref[0]
    d = x.shape[1]
    shift = mod_ref[0, 0, 0:1, :]
    scale = mod_ref[0, 0, 1:2, :]
    gate = mod_ref[0, 0, 2:3, :]
    h = (_rms(x) * ng_ref[...] * (1.0 + scale) + shift).astype(BF16)
    acc = jnp.zeros(x.shape, F32)
    for r, y_ref in enumerate((ya_ref, yb_ref, yc_ref)):
        w = y_ref.shape[2]
        z = jnp.dot(h, wz_ref[:, r * w:(r + 1) * w], preferred_element_type=F32)
        u = (y_ref[0].astype(F32) * (z * jax.nn.sigmoid(z))).astype(BF16)
        tr = jnp.dot(u, wb_ref[r], preferred_element_type=F32)
        gm = jnp.dot(h, wg_ref[:, r * d:(r + 1) * d], preferred_element_type=F32)
        acc = acc + jax.nn.sigmoid(gm) * tr
    out = jnp.dot(acc.astype(BF16), wo_ref[...], preferred_element_type=F32)
    xn = x + gate * out
    if final:
        xn = _rms(xn) * fn_ref[...]
    o_ref[0] = xn


def _merge(x_all, mod4, norm_g, ya, yb, yc, wts, *, ctx_len, lat0, need_ctx, final_norm=None):
    b, t, d = x_all.shape
    tm = MERGE_TM
    lat_blk = lat0 // tm
    final = final_norm is not None
    if need_ctx:
        n_ctx_blk = ctx_len // tm
        nblk = (t - lat0 + ctx_len) // tm
        blk_of = lambda i: jnp.where(i < n_ctx_blk, i, i + (lat_blk - n_ctx_blk))
    else:
        nblk = (t - lat0) // tm
        blk_of = lambda i: i + lat_blk
    row_spec = pl.BlockSpec((1, tm, d), lambda bi, i: (bi, blk_of(i), 0))
    in_specs = [
        row_spec,
        pl.BlockSpec((1, 1, 3, d), lambda bi, i: (bi, jnp.where(blk_of(i) >= lat_blk, 1, 0), 0, 0)),
        _const_spec((1, d)),
        row_spec, row_spec, row_spec,
        _const_spec(wts["wz"].shape), _const_spec(wts["wg"].shape),
        _const_spec(wts["wb"].shape), _const_spec(wts["wo"].shape),
    ]
    args = [x_all, mod4, norm_g, ya, yb, yc, wts["wz"], wts["wg"], wts["wb"], wts["wo"]]
    if final:
        in_specs.append(_const_spec((1, d)))
        args.append(final_norm)
        out_shape = jax.ShapeDtypeStruct((b, t - lat0, d), F32)
        out_spec = pl.BlockSpec((1, tm, d), lambda bi, i: (bi, i, 0))
        aliases = {}
    else:
        out_shape = jax.ShapeDtypeStruct((b, t, d), F32)
        out_spec = row_spec
        aliases = {0: 0}
    return pl.pallas_call(
        functools.partial(_merge_kernel, final=final),
        grid=(b, nblk),
        in_specs=in_specs,
        out_specs=out_spec,
        out_shape=out_shape,
        input_output_aliases=aliases,
        compiler_params=pltpu.CompilerParams(
            dimension_semantics=("parallel", "arbitrary"), vmem_limit_bytes=VMEM_LIMIT),
    )(*args)


def _layer_weights(l, w_in, mla_q_norm, mla_w_uq, mla_kv_norm, mla_w_ukv, w_branch, w_out):
    d = w_in.shape[1]
    splits = (MLA_Q_RANK, MLA_KV_RANK, MLA_ROPE,
              DIFF_HEADS * 2 * DIFF_HD, DIFF_HEADS * 2 * DIFF_HD, DIFF_HEADS * DIFF_V,
              SWA_HEADS * SWA_HD, SWA_KV_HEADS * SWA_HD, SWA_KV_HEADS * SWA_HD,
              N_BRANCH * w_branch.shape[2], N_BRANCH * d)
    offs = [0]
    for s in splits:
        offs.append(offs[-1] + s)
    w = w_in[l]
    col = lambda a, b_: w[:, offs[a]:offs[b_]]
    zeros = lambda n: jnp.zeros((d, n), w.dtype)
    w1 = jnp.concatenate([col(0, 2), zeros(MLA_NOPE), col(2, 3), zeros(LANE - MLA_NOPE - MLA_ROPE)], axis=1)
    qk = MLA_NOPE + MLA_ROPE
    wuq = jnp.pad(mla_w_uq[l].reshape(MLA_Q_RANK, MLA_HEADS, qk), ((0, 0), (0, 0), (0, LANE - qk)))
    wukv = mla_w_ukv[l].reshape(MLA_KV_RANK, MLA_HEADS, MLA_NOPE + MLA_V)
    wuk = jnp.pad(wukv[:, :, :MLA_NOPE], ((0, 0), (0, 0), (0, LANE - MLA_NOPE)))
    wuv = wukv[:, :, MLA_NOPE:]
    return {
        "w1": w1.astype(BF16),
        "qn": mla_q_norm[l].reshape(1, -1),
        "kvn": mla_kv_norm[l].reshape(1, -1),
        "wuq": wuq.reshape(MLA_Q_RANK, MLA_HEADS * LANE).astype(BF16),
        "wukv": jnp.concatenate([wuk.reshape(MLA_KV_RANK, -1), wuv.reshape(MLA_KV_RANK, -1)], axis=1).astype(BF16),
        "wd": col(3, 6).astype(BF16),
        "ws": col(6, 9).astype(BF16),
        "wz": col(9, 10).astype(BF16),
        "wg": col(10, 11).astype(BF16),
        "wb": w_branch[l].astype(BF16),
        "wo": w_out[l].astype(BF16),
    }


def _rope_tables(n_lat, lat0):
    rows = n_lat // GRID_W
    row = jnp.repeat(jnp.arange(rows), GRID_W).astype(F32)
    col = jnp.tile(jnp.arange(GRID_W), rows).astype(F32)

    def cs(rot_dim):
        n_freq = rot_dim // 4
        inv_freq = ROPE_BASE ** (-jnp.arange(n_freq, dtype=F32) / n_freq)
        ang = jnp.concatenate([row[:, None] * inv_freq, col[:, None] * inv_freq], axis=-1)
        cos = jnp.concatenate([jnp.ones((lat0, rot_dim // 2), F32), jnp.cos(ang)], axis=0)
        sin = jnp.concatenate([jnp.zeros((lat0, rot_dim // 2), F32), jnp.sin(ang)], axis=0)
        return cos, sin

    cos, sin = cs(SWA_HD)
    cch = jnp.tile(cos, (1, LANE // (SWA_HD // 2)))
    ssh = jnp.tile(jnp.concatenate([-sin, sin], axis=1), (1, LANE // SWA_HD))
    cos, sin = cs(MLA_ROPE)
    t = cos.shape[0]
    pad = LANE - MLA_NOPE - MLA_ROPE
    ccm = jnp.concatenate([jnp.ones((t, MLA_NOPE), F32), cos, cos, jnp.zeros((t, pad), F32)], axis=1)
    ssm = jnp.concatenate([jnp.zeros((t, MLA_NOPE), F32), -sin, sin, jnp.zeros((t, pad), F32)], axis=1)
    return {"cch": cch, "ssh": ssh, "ccm": ccm, "ssm": ssm}


def kernel(x, c, ctx, c_ctx, w_mod, b_mod, norm_g, w_in, mla_q_norm, mla_w_uq, mla_kv_norm, mla_w_ukv,
           diff_lq1, diff_lk1, diff_lq2, diff_lk2, diff_subln, swa_sink, w_branch, w_out, final_norm):
    b, n_lat, d = x.shape
    ctx_len = ctx.shape[1]
    depth = w_mod.shape[0]
    lat0 = -(-ctx_len // FLASH_TQ) * FLASH_TQ
    assert ctx_len % SWA_TQ == 0 and ctx_len % MERGE_TM == 0 and n_lat % FLASH_TK == 0 and n_lat % GRID_W == 0

    x_all = jnp.concatenate([ctx, jnp.zeros((b, lat0 - ctx_len, d), x.dtype), x], axis=1)
    tabs = _rope_tables(n_lat, lat0)

    c_rows = jnp.concatenate([c, c_ctx[None, :], jnp.zeros((16 - b - 1, d), c.dtype)], axis=0)
    mod = _modulation(c_rows, w_mod, b_mod)

    out = None
    for l in range(depth):
        need_ctx = l < depth - 1
        mod_lat = mod[l, :b].reshape(b, 1, 3, d)
        mod_ctx = jnp.broadcast_to(mod[l, b].reshape(1, 1, 3, d), (b, 1, 3, d))
        mod4 = jnp.concatenate([mod_ctx, mod_lat], axis=1)
        wts = _layer_weights(l, w_in, mla_q_norm, mla_w_uq, mla_kv_norm, mla_w_ukv, w_branch, w_out)
        ng = norm_g[l].reshape(1, d)

        mqt, mk, mvt, dqt, dk, dvt, sqt, sk, svt = _project(x_all, mod4, ng, wts, tabs, lat0)

        lam_init = 0.8 - 0.6 * math.exp(-0.3 * l)
        lam = (jnp.exp(jnp.sum(diff_lq1[l].astype(F32) * diff_lk1[l].astype(F32)))
               - jnp.exp(jnp.sum(diff_lq2[l].astype(F32) * diff_lk2[l].astype(F32))) + lam_init).reshape(1)

        ya = _flash(mqt, mk, mvt, n_sub=1, ctx_len=ctx_len, lat0=lat0, need_ctx=need_ctx)
        yb = _flash(dqt, dk, dvt, n_sub=2, ctx_len=ctx_len, lat0=lat0, need_ctx=need_ctx,
                    lam=lam, subln=diff_subln[l].reshape(LANE, 1), lam_scale=1.0 - lam_init)
        yc = _swa(swa_sink[l], sqt, sk, svt, ctx_len=ctx_len, lat0=lat0, need_ctx=need_ctx)

        if need_ctx:
            x_all = _merge(x_all, mod4, ng, ya, yb, yc, wts, ctx_len=ctx_len, lat0=lat0, need_ctx=True)
        else:
            out = _merge(x_all, mod4, ng, ya, yb, yc, wts, ctx_len=ctx_len, lat0=lat0, need_ctx=False,
                         final_norm=final_norm.reshape(1, d))
    return out
```

```python
import functools
import math

import jax
import jax.numpy as jnp
from jax import lax
from jax.experimental import pallas as pl
from jax.experimental.pallas import tpu as pltpu

F32 = jnp.float32
BF16 = jnp.bfloat16

GRID_W = 64
N_BRANCH = 3
EPS = 1e-6
ROPE_BASE = 10000.0
NEG_INF = -1e30
LOG2E = math.log2(math.e)

MLA_HEADS = 8
MLA_NOPE = 64
MLA_ROPE = 32
MLA_V = 128
MLA_Q_RANK = 384
MLA_KV_RANK = 256

DIFF_HEADS = 8
DIFF_HD = 64
DIFF_V = 2 * DIFF_HD

SWA_HEADS = 16
SWA_KV_HEADS = 4
SWA_GROUP = SWA_HEADS // SWA_KV_HEADS
SWA_HD = 64
WINDOW = 128

LANE = 128

VMEM_LIMIT = 56 * 1024 * 1024

PROJ_TM = 256
MERGE_TM = 256
FLASH_TQ = 512
FLASH_TK = 512
SWA_TQ = 256


def _const_spec(shape):
    zeros = (0,) * len(shape)
    return pl.BlockSpec(shape, lambda *_: zeros, pipeline_mode=pl.Buffered(1))


def _rms(x, eps=EPS):
    return x * lax.rsqrt(jnp.mean(x * x, axis=-1, keepdims=True) + eps)


def _mod_kernel(c_ref, w_ref, b_ref, o_ref):
    cv = c_ref[...]
    s = cv * jax.nn.sigmoid(cv)
    o_ref[0] = jnp.dot(s, w_ref[0], preferred_element_type=F32) + b_ref[0]


def _modulation(c_rows, w_mod, b_mod):
    depth, d, d3 = w_mod.shape
    rows = c_rows.shape[0]
    tn = d
    return pl.pallas_call(
        _mod_kernel,
        grid=(depth, d3 // tn),
        in_specs=[
            pl.BlockSpec((rows, d), lambda l, n: (0, 0)),
            pl.BlockSpec((1, d, tn), lambda l, n: (l, 0, n)),
            pl.BlockSpec((1, 1, tn), lambda l, n: (l, 0, n)),
        ],
        out_specs=pl.BlockSpec((1, rows, tn), lambda l, n: (l, 0, n)),
        out_shape=jax.ShapeDtypeStruct((depth, rows, d3), F32),
        name="modulation",
        compiler_params=pltpu.CompilerParams(
            dimension_semantics=("arbitrary", "arbitrary"), vmem_limit_bytes=VMEM_LIMIT),
    )(c_rows, w_mod, b_mod.reshape(depth, 1, d3))


def _rope(x, cc, ss, lo_mask, half):
    partner = jnp.where(lo_mask, pltpu.roll(x, LANE - half, axis=1), pltpu.roll(x, half, axis=1))
    return x * cc + partner * ss


def _proj_kernel(x_ref, mod_ref, ng_ref, w1_ref, qn_ref, kvn_ref, wuq_ref, wukv_ref, wd_ref, ws_ref,
                 cch_ref, ssh_ref, ccm_ref, ssm_ref,
                 mqt_ref, mk_ref, mvt_ref, dqt_ref, dk_ref, dvt_ref, sqt_ref, sk_ref, svt_ref,
                 *, mla_scale, hd_scale):
    x = x_ref[0]
    shift = mod_ref[0, 0, 0:1, :]
    scale = mod_ref[0, 0, 1:2, :]
    h = (_rms(x) * ng_ref[...] * (1.0 + scale) + shift).astype(BF16)
    tm = x.shape[0]

    lane = lax.broadcasted_iota(jnp.int32, (tm, LANE), 1)
    lo_hd = (lane & (SWA_HD // 2)) == 0
    lo_mla = (lane & (MLA_ROPE // 2)) == 0
    cch, ssh = cch_ref[...], ssh_ref[...]
    ccm, ssm = ccm_ref[...], ssm_ref[...]

    t1 = jnp.dot(h, w1_ref[...], preferred_element_type=F32)
    qc = t1[:, :MLA_Q_RANK]
    kvc = t1[:, MLA_Q_RANK:MLA_Q_RANK + MLA_KV_RANK]
    kr = t1[:, MLA_Q_RANK + MLA_KV_RANK:]
    qn = (_rms(qc) * qn_ref[...]).astype(BF16)
    kvn = (_rms(kvc) * kvn_ref[...]).astype(BF16)
    q_m = jnp.dot(qn, wuq_ref[...], preferred_element_type=F32)
    kv_m = jnp.dot(kvn, wukv_ref[...], preferred_element_type=F32)
    kr_roped = _rope(kr, ccm, ssm, lo_mla, MLA_ROPE // 2)
    nh = MLA_HEADS * LANE
    for hh in range(MLA_HEADS):
        sl = slice(hh * LANE, (hh + 1) * LANE)
        qh = _rope(q_m[:, sl], ccm, ssm, lo_mla, MLA_ROPE // 2) * mla_scale
        mqt_ref[0, hh] = qh.T.astype(BF16)
        mk_ref[0, hh] = (kv_m[:, sl] + kr_roped).astype(BF16)
        mvt_ref[0, hh] = kv_m[:, nh + hh * LANE:nh + (hh + 1) * LANE].T.astype(BF16)

    t2 = jnp.dot(h, wd_ref[...], preferred_element_type=F32)
    nd = DIFF_HEADS * LANE
    for hh in range(DIFF_HEADS):
        sl = slice(hh * LANE, (hh + 1) * LANE)
        qh = _rope(t2[:, sl], cch, ssh, lo_hd, DIFF_HD // 2) * hd_scale
        dqt_ref[0, hh] = qh.T.astype(BF16)
        kh = _rope(t2[:, nd + hh * LANE:nd + (hh + 1) * LANE], cch, ssh, lo_hd, DIFF_HD // 2)
        dk_ref[0, hh] = kh.astype(BF16)
        dvt_ref[0, hh] = t2[:, 2 * nd + hh * LANE:2 * nd + (hh + 1) * LANE].T.astype(BF16)

    t3 = jnp.dot(h, ws_ref[...], preferred_element_type=F32)
    nq = SWA_HEADS * SWA_HD
    nkv = SWA_KV_HEADS * SWA_HD
    for j in range(nq // LANE):
        qt = (_rope(t3[:, j * LANE:(j + 1) * LANE], cch, ssh, lo_hd, SWA_HD // 2) * hd_scale).T.astype(BF16)
        sqt_ref[0, 2 * j] = qt[:SWA_HD]
        sqt_ref[0, 2 * j + 1] = qt[SWA_HD:]
    for j in range(nkv // LANE):
        kh = _rope(t3[:, nq + j * LANE:nq + (j + 1) * LANE], cch, ssh, lo_hd, SWA_HD // 2).astype(BF16)
        sk_ref[0, 2 * j] = kh[:, :SWA_HD]
        sk_ref[0, 2 * j + 1] = kh[:, SWA_HD:]
        vt = t3[:, nq + nkv + j * LANE:nq + nkv + (j + 1) * LANE].T.astype(BF16)
        svt_ref[0, 2 * j] = vt[:SWA_HD]
        svt_ref[0, 2 * j + 1] = vt[SWA_HD:]


def _project(x_all, mod4, norm_g, wts, tabs, lat0):
    b, t, d = x_all.shape
    tm = PROJ_TM
    lat_blk = lat0 // tm
    mla_scale = (MLA_NOPE + MLA_ROPE) ** -0.5 * LOG2E
    hd_scale = SWA_HD ** -0.5 * LOG2E

    def headmajor(nh, natural, width=LANE):
        if natural:
            return (jax.ShapeDtypeStruct((b, nh, t, width), BF16),
                    pl.BlockSpec((1, nh, tm, width), lambda bi, i: (bi, 0, i, 0)))
        return (jax.ShapeDtypeStruct((b, nh, width, t), BF16),
                pl.BlockSpec((1, nh, width, tm), lambda bi, i: (bi, 0, 0, i)))

    outs = [headmajor(MLA_HEADS, False), headmajor(MLA_HEADS, True), headmajor(MLA_HEADS, False),
            headmajor(DIFF_HEADS, False), headmajor(DIFF_HEADS, True), headmajor(DIFF_HEADS, False),
            headmajor(SWA_HEADS, False, SWA_HD), headmajor(SWA_KV_HEADS, True, SWA_HD),
            headmajor(SWA_KV_HEADS, False, SWA_HD)]
    tab_spec = pl.BlockSpec((tm, LANE), lambda bi, i: (i, 0))
    in_specs = [
        pl.BlockSpec((1, tm, d), lambda bi, i: (bi, i, 0)),
        pl.BlockSpec((1, 1, 3, d), lambda bi, i: (bi, jnp.where(i >= lat_blk, 1, 0), 0, 0)),
        _const_spec((1, d)),
        _const_spec(wts["w1"].shape), _const_spec((1, MLA_Q_RANK)), _const_spec((1, MLA_KV_RANK)),
        _const_spec(wts["wuq"].shape), _const_spec(wts["wukv"].shape),
        _const_spec(wts["wd"].shape), _const_spec(wts["ws"].shape),
        tab_spec, tab_spec, tab_spec, tab_spec,
    ]
    return pl.pallas_call(
        functools.partial(_proj_kernel, mla_scale=mla_scale, hd_scale=hd_scale),
        grid=(b, t // tm),
        in_specs=in_specs,
        out_specs=[o[1] for o in outs],
        out_shape=[o[0] for o in outs],
        name="proj",
        compiler_params=pltpu.CompilerParams(
            dimension_semantics=("parallel", "arbitrary"), vmem_limit_bytes=VMEM_LIMIT),
    )(x_all, mod4, norm_g, wts["w1"], wts["qn"], wts["kvn"], wts["wuq"], wts["wukv"], wts["wd"], wts["ws"],
      tabs["cch"], tabs["ssh"], tabs["ccm"], tabs["ssm"])


def _flash_kernel(*refs, n_sub, tq, tk, ctx_len, lat0, n_chunks, first_blk, lam_scale):
    if n_sub == 2:
        lam_ref, qt_ref, k_ref, vt_ref, subln_ref, o_ref, acc_ref, m_ref, l_ref, s_ref = refs
    else:
        qt_ref, k_ref, vt_ref, o_ref, acc_ref, m_ref, l_ref, s_ref = refs
    qt = qt_ref[0, 0]
    if n_sub == 2:
        row = lax.broadcasted_iota(jnp.int32, qt.shape, 0)
        zero = jnp.zeros_like(qt)
        qt = jnp.concatenate([jnp.where(row < DIFF_HD, qt, zero), jnp.where(row >= DIFF_HD, qt, zero)], axis=1)

    def scores(start, size):
        kc = k_ref[0, 0, pl.ds(start, size), :]
        return jnp.dot(kc, qt, preferred_element_type=F32)

    def pv(start, size, p):
        vc = vt_ref[0, 0, :, pl.ds(start, size)]
        return jnp.dot(vc, p.astype(BF16), preferred_element_type=F32)

    def process(s, m, l, start, size):
        m_new = jnp.maximum(m, jnp.max(s, axis=0, keepdims=True))
        p = jnp.exp2(s - m_new)
        alpha = jnp.exp2(m - m_new)
        l_new = alpha * l + jnp.sum(p, axis=0, keepdims=True)
        acc_ref[...] = alpha * acc_ref[...] + pv(start, size, p)
        return m_new, l_new

    s = scores(0, ctx_len)
    m0 = jnp.max(s, axis=0, keepdims=True)
    p = jnp.exp2(s - m0)
    m_ref[...] = m0
    l_ref[...] = jnp.sum(p, axis=0, keepdims=True)
    acc_ref[...] = pv(0, ctx_len, p)

    def chunk_start(c):
        return pl.multiple_of(lat0 + c * tk, tk)

    def pair(c2, m, l, last):
        c = 2 * c2
        s_ref[1] = scores(chunk_start(c + 1), tk)
        m, l = process(s_ref[0], m, l, chunk_start(c), tk)
        if not last:
            s_ref[0] = scores(chunk_start(c + 2), tk)
        return process(s_ref[1], m, l, chunk_start(c + 1), tk)

    @pl.when(pl.program_id(2) + first_blk > 0)
    def _():
        s_ref[0] = scores(chunk_start(0), tk)
        n_pairs = n_chunks // 2
        m, l = lax.fori_loop(0, n_pairs - 1, lambda c2, ml: pair(c2, ml[0], ml[1], False),
                             (m_ref[...], l_ref[...]))
        m, l = pair(n_pairs - 1, m, l, True)
        m_ref[...] = m
        l_ref[...] = l

    o = acc_ref[...] / l_ref[...]
    if n_sub == 2:
        lam = lam_ref[0]
        d = o[:, :tq] - lam * o[:, tq:]
        y = d * lax.rsqrt(jnp.mean(d * d, axis=0, keepdims=True) + EPS) * subln_ref[...] * lam_scale
    else:
        y = o
    o_ref[0] = y.T.astype(o_ref.dtype)


def _flash(qt, k, vt, *, n_sub, ctx_len, lat0, need_ctx, lam=None, subln=None, lam_scale=1.0):
    b, nh, _, t = qt.shape
    tq, tk = FLASH_TQ, FLASH_TK
    n_chunks = (t - lat0) // tk
    first_blk = 0 if need_ctx else lat0 // tq
    nq = t // tq - first_blk
    in_specs = [
        pl.BlockSpec((1, 1, LANE, tq), lambda bi, hi, i: (bi, hi, 0, i + first_blk)),
        pl.BlockSpec((1, 1, t, LANE), lambda bi, hi, i: (bi, hi, 0, 0)),
        pl.BlockSpec((1, 1, LANE, t), lambda bi, hi, i: (bi, hi, 0, 0)),
    ]
    args = [qt, k, vt]
    if n_sub == 2:
        in_specs = [pl.BlockSpec(memory_space=pltpu.SMEM)] + in_specs + [_const_spec((LANE, 1))]
        args = [lam] + args + [subln]
    return pl.pallas_call(
        functools.partial(_flash_kernel, n_sub=n_sub, tq=tq, tk=tk, ctx_len=ctx_len, lat0=lat0,
                          n_chunks=n_chunks, first_blk=first_blk, lam_scale=lam_scale),
        grid=(b, nh, nq),
        in_specs=in_specs,
        out_specs=pl.BlockSpec((1, tq, LANE), lambda bi, hi, i: (bi, i + first_blk, hi)),
        out_shape=jax.ShapeDtypeStruct((b, t, nh * LANE), BF16),
        scratch_shapes=[pltpu.VMEM((LANE, n_sub * tq), F32), pltpu.VMEM((1, n_sub * tq), F32),
                        pltpu.VMEM((1, n_sub * tq), F32), pltpu.VMEM((2, tk, n_sub * tq), F32)],
        name="flash_diff" if n_sub == 2 else "flash_mla",
        compiler_params=pltpu.CompilerParams(
            dimension_semantics=("parallel", "parallel", "arbitrary"), vmem_limit_bytes=VMEM_LIMIT),
    )(*args)


def _swa_kernel(sink_ref, qt_ref, k_ref, vt_ref, o_ref, *, tq, ctx_len, lat0, t_total, blk_of):
    g = pl.program_id(1)
    qs = blk_of(pl.program_id(2)) * tq
    span = tq + 2 * WINDOW
    ws = pl.multiple_of(jnp.clip(qs - WINDOW, 0, t_total - span), LANE)

    qt = jnp.concatenate([qt_ref[0, j] for j in range(SWA_GROUP)], axis=1)
    s_ctx = jnp.dot(k_ref[0, 0, 0:ctx_len, :], qt, preferred_element_type=F32)
    s_loc = jnp.dot(k_ref[0, 0, pl.ds(ws, span), :], qt, preferred_element_type=F32)

    kpos = ws - lat0 + lax.broadcasted_iota(jnp.int32, (span, tq), 0)
    qpos = qs - lat0 + lax.broadcasted_iota(jnp.int32, (span, tq), 1)
    mask = (jnp.abs(qpos - kpos) <= WINDOW) & (kpos >= 0)
    mask = jnp.concatenate([mask] * SWA_GROUP, axis=1)
    s_loc = jnp.where(mask, s_loc, NEG_INF)

    sink = jnp.concatenate(
        [jnp.full((1, tq), sink_ref[g * SWA_GROUP + j] * LOG2E, F32) for j in range(SWA_GROUP)], axis=1)
    m = jnp.maximum(jnp.maximum(jnp.max(s_ctx, axis=0, keepdims=True), jnp.max(s_loc, axis=0, keepdims=True)), sink)
    p_ctx = jnp.exp2(s_ctx - m)
    p_loc = jnp.exp2(s_loc - m)
    l = jnp.sum(p_ctx, axis=0, keepdims=True) + jnp.sum(p_loc, axis=0, keepdims=True) + jnp.exp2(sink - m)
    ot = (jnp.dot(vt_ref[0, 0, :, 0:ctx_len], p_ctx.astype(BF16), preferred_element_type=F32)
          + jnp.dot(vt_ref[0, 0, :, pl.ds(ws, span)], p_loc.astype(BF16), preferred_element_type=F32))
    ot = ot / l
    o_ref[0] = jnp.concatenate([ot[:, j * tq:(j + 1) * tq].T for j in range(SWA_GROUP)], axis=1).astype(o_ref.dtype)


def _swa(sink, qt, k, vt, *, ctx_len, lat0, need_ctx):
    b, _, _, t = qt.shape
    tq = SWA_TQ
    lat_blk = lat0 // tq
    if need_ctx:
        nq = (t - lat0 + ctx_len) // tq
        n_ctx_blk = ctx_len // tq
        blk_of = lambda i: jnp.where(i < n_ctx_blk, i, i + (lat_blk - n_ctx_blk))
    else:
        nq = (t - lat0) // tq
        blk_of = lambda i: i + lat_blk
    return pl.pallas_call(
        functools.partial(_swa_kernel, tq=tq, ctx_len=ctx_len, lat0=lat0, t_total=t, blk_of=blk_of),
        grid=(b, SWA_KV_HEADS, nq),
        in_specs=[
            pl.BlockSpec(memory_space=pltpu.SMEM),
            pl.BlockSpec((1, SWA_GROUP, SWA_HD, tq), lambda bi, gi, i: (bi, gi, 0, blk_of(i))),
            pl.BlockSpec((1, 1, t, SWA_HD), lambda bi, gi, i: (bi, gi, 0, 0)),
            pl.BlockSpec((1, 1, SWA_HD, t), lambda bi, gi, i: (bi, gi, 0, 0)),
        ],
        out_specs=pl.BlockSpec((1, tq, SWA_GROUP * SWA_HD), lambda bi, gi, i: (bi, blk_of(i), gi)),
        out_shape=jax.ShapeDtypeStruct((b, t, SWA_HEADS * SWA_HD), BF16),
        name="swa",
        compiler_params=pltpu.CompilerParams(
            dimension_semantics=("parallel", "parallel", "arbitrary"), vmem_limit_bytes=VMEM_LIMIT),
    )(sink, qt, k, vt)


def _merge_kernel(*refs, final):
    if final:
        x_ref, mod_ref, ng_ref, ya_ref, yb_ref, yc_ref, wz_ref, wg_ref, wb_ref, wo_ref, fn_ref, o_ref = refs
    else:
        x_ref, mod_ref, ng_ref, ya_ref, yb_ref, yc_ref, wz_ref, wg_ref, wb_ref, wo_ref, o_ref = refs
    x = x_ref[0]
    d = x.shape[1]
    shift = mod_ref[0, 0, 0:1, :]
    scale = mod_ref[0, 0, 1:2, :]
    gate = mod_ref[0, 0, 2:3, :]
    h = (_rms(x) * ng_ref[...] * (1.0 + scale) + shift).astype(BF16)
    acc = jnp.zeros(x.shape, F32)
    for r, y_ref in enumerate((ya_ref, yb_ref, yc_ref)):
        w = y_ref.shape[2]
        z = jnp.dot(h, wz_ref[:, r * w:(r + 1) * w], preferred_element_type=F32)
        u = (y_ref[0].astype(F32) * (z * jax.nn.sigmoid(z))).astype(BF16)
        tr = jnp.dot(u, wb_ref[r], preferred_element_type=F32)
        gm = jnp.dot(h, wg_ref[:, r * d:(r + 1) * d], preferred_element_type=F32)
        acc = acc + jax.nn.sigmoid(gm) * tr
    out = jnp.dot(acc.astype(BF16), wo_ref[...], preferred_element_type=F32)
    xn = x + gate * out
    if final:
        xn = _rms(xn) * fn_ref[...]
    o_ref[0] = xn


def _merge(x_all, mod4, norm_g, ya, yb, yc, wts, *, ctx_len, lat0, need_ctx, final_norm=None):
    b, t, d = x_all.shape
    tm = MERGE_TM
    lat_blk = lat0 // tm
    final = final_norm is not None
    if need_ctx:
        n_ctx_blk = ctx_len // tm
        nblk = (t - lat0 + ctx_len) // tm
        blk_of = lambda i: jnp.where(i < n_ctx_blk, i, i + (lat_blk - n_ctx_blk))
    else:
        nblk = (t - lat0) // tm
        blk_of = lambda i: i + lat_blk
    row_spec = pl.BlockSpec((1, tm, d), lambda bi, i: (bi, blk_of(i), 0))
    in_specs = [
        row_spec,
        pl.BlockSpec((1, 1, 3, d), lambda bi, i: (bi, jnp.where(blk_of(i) >= lat_blk, 1, 0), 0, 0)),
        _const_spec((1, d)),
        row_spec, row_spec, row_spec,
        _const_spec(wts["wz"].shape), _const_spec(wts["wg"].shape),
        _const_spec(wts["wb"].shape), _const_spec(wts["wo"].shape),
    ]
    args = [x_all, mod4, norm_g, ya, yb, yc, wts["wz"], wts["wg"], wts["wb"], wts["wo"]]
    if final:
        in_specs.append(_const_spec((1, d)))
        args.append(final_norm)
        out_shape = jax.ShapeDtypeStruct((b, t - lat0, d), F32)
        out_spec = pl.BlockSpec((1, tm, d), lambda bi, i: (bi, i, 0))
        aliases = {}
    else:
        out_shape = jax.ShapeDtypeStruct((b, t, d), F32)
        out_spec = row_spec
        aliases = {0: 0}
    return pl.pallas_call(
        functools.partial(_merge_kernel, final=final),
        grid=(b, nblk),
        in_specs=in_specs,
        out_specs=out_spec,
        out_shape=out_shape,
        input_output_aliases=aliases,
        name="merge",
        compiler_params=pltpu.CompilerParams(
            dimension_semantics=("parallel", "arbitrary"), vmem_limit_bytes=VMEM_LIMIT),
    )(*args)


def _layer_weights(l, w_in, mla_q_norm, mla_w_uq, mla_kv_norm, mla_w_ukv, w_branch, w_out):
    d = w_in.shape[1]
    splits = (MLA_Q_RANK, MLA_KV_RANK, MLA_ROPE,
              DIFF_HEADS * 2 * DIFF_HD, DIFF_HEADS * 2 * DIFF_HD, DIFF_HEADS * DIFF_V,
              SWA_HEADS * SWA_HD, SWA_KV_HEADS * SWA_HD, SWA_KV_HEADS * SWA_HD,
              N_BRANCH * w_branch.shape[2], N_BRANCH * d)
    offs = [0]
    for s in splits:
        offs.append(offs[-1] + s)
    w = w_in[l]
    col = lambda a, b_: w[:, offs[a]:offs[b_]]
    zeros = lambda n: jnp.zeros((d, n), w.dtype)
    w1 = jnp.concatenate([col(0, 2), zeros(MLA_NOPE), col(2, 3), zeros(LANE - MLA_NOPE - MLA_ROPE)], axis=1)
    qk = MLA_NOPE + MLA_ROPE
    wuq = jnp.pad(mla_w_uq[l].reshape(MLA_Q_RANK, MLA_HEADS, qk), ((0, 0), (0, 0), (0, LANE - qk)))
    wukv = mla_w_ukv[l].reshape(MLA_KV_RANK, MLA_HEADS, MLA_NOPE + MLA_V)
    wuk = jnp.pad(wukv[:, :, :MLA_NOPE], ((0, 0), (0, 0), (0, LANE - MLA_NOPE)))
    wuv = wukv[:, :, MLA_NOPE:]
    return {
        "w1": w1.astype(BF16),
        "qn": mla_q_norm[l].reshape(1, -1),
        "kvn": mla_kv_norm[l].reshape(1, -1),
        "wuq": wuq.reshape(MLA_Q_RANK, MLA_HEADS * LANE).astype(BF16),
        "wukv": jnp.concatenate([wuk.reshape(MLA_KV_RANK, -1), wuv.reshape(MLA_KV_RANK, -1)], axis=1).astype(BF16),
        "wd": col(3, 6).astype(BF16),
        "ws": col(6, 9).astype(BF16),
        "wz": col(9, 10).astype(BF16),
        "wg": col(10, 11).astype(BF16),
        "wb": w_branch[l].astype(BF16),
        "wo": w_out[l].astype(BF16),
    }


def _rope_tables(n_lat, lat0):
    rows = n_lat // GRID_W
    row = jnp.repeat(jnp.arange(rows), GRID_W).astype(F32)
    col = jnp.tile(jnp.arange(GRID_W), rows).astype(F32)

    def cs(rot_dim):
        n_freq = rot_dim // 4
        inv_freq = ROPE_BASE ** (-jnp.arange(n_freq, dtype=F32) / n_freq)
        ang = jnp.concatenate([row[:, None] * inv_freq, col[:, None] * inv_freq], axis=-1)
        cos = jnp.concatenate([jnp.ones((lat0, rot_dim // 2), F32), jnp.cos(ang)], axis=0)
        sin = jnp.concatenate([jnp.zeros((lat0, rot_dim // 2), F32), jnp.sin(ang)], axis=0)
        return cos, sin

    cos, sin = cs(SWA_HD)
    cch = jnp.tile(cos, (1, LANE // (SWA_HD // 2)))
    ssh = jnp.tile(jnp.concatenate([-sin, sin], axis=1), (1, LANE // SWA_HD))
    cos, sin = cs(MLA_ROPE)
    t = cos.shape[0]
    pad = LANE - MLA_NOPE - MLA_ROPE
    ccm = jnp.concatenate([jnp.ones((t, MLA_NOPE), F32), cos, cos, jnp.zeros((t, pad), F32)], axis=1)
    ssm = jnp.concatenate([jnp.zeros((t, MLA_NOPE), F32), -sin, sin, jnp.zeros((t, pad), F32)], axis=1)
    return {"cch": cch, "ssh": ssh, "ccm": ccm, "ssm": ssm}


def kernel(x, c, ctx, c_ctx, w_mod, b_mod, norm_g, w_in, mla_q_norm, mla_w_uq, mla_kv_norm, mla_w_ukv,
           diff_lq1, diff_lk1, diff_lq2, diff_lk2, diff_subln, swa_sink, w_branch, w_out, final_norm):
    b, n_lat, d = x.shape
    ctx_len = ctx.shape[1]
    depth = w_mod.shape[0]
    lat0 = -(-ctx_len // FLASH_TQ) * FLASH_TQ
    assert ctx_len % SWA_TQ == 0 and ctx_len % MERGE_TM == 0 and n_lat % FLASH_TK == 0 and n_lat % GRID_W == 0

    x_all = jnp.concatenate([ctx, jnp.zeros((b, lat0 - ctx_len, d), x.dtype), x], axis=1)
    tabs = _rope_tables(n_lat, lat0)

    c_rows = jnp.concatenate([c, c_ctx[None, :], jnp.zeros((16 - b - 1, d), c.dtype)], axis=0)
    mod = _modulation(c_rows, w_mod, b_mod)

    out = None
    for l in range(depth):
        need_ctx = l < depth - 1
        mod_lat = mod[l, :b].reshape(b, 1, 3, d)
        mod_ctx = jnp.broadcast_to(mod[l, b].reshape(1, 1, 3, d), (b, 1, 3, d))
        mod4 = jnp.concatenate([mod_ctx, mod_lat], axis=1)
        wts = _layer_weights(l, w_in, mla_q_norm, mla_w_uq, mla_kv_norm, mla_w_ukv, w_branch, w_out)
        ng = norm_g[l].reshape(1, d)

        mqt, mk, mvt, dqt, dk, dvt, sqt, sk, svt = _project(x_all, mod4, ng, wts, tabs, lat0)

        lam_init = 0.8 - 0.6 * math.exp(-0.3 * l)
        lam = (jnp.exp(jnp.sum(diff_lq1[l].astype(F32) * diff_lk1[l].astype(F32)))
               - jnp.exp(jnp.sum(diff_lq2[l].astype(F32) * diff_lk2[l].astype(F32))) + lam_init).reshape(1)

        ya = _flash(mqt, mk, mvt, n_sub=1, ctx_len=ctx_len, lat0=lat0, need_ctx=need_ctx)
        yb = _flash(dqt, dk, dvt, n_sub=2, ctx_len=ctx_len, lat0=lat0, need_ctx=need_ctx,
                    lam=lam, subln=diff_subln[l].reshape(LANE, 1), lam_scale=1.0 - lam_init)
        yc = _swa(swa_sink[l], sqt, sk, svt, ctx_len=ctx_len, lat0=lat0, need_ctx=need_ctx)

        if need_ctx:
            x_all = _merge(x_all, mod4, ng, ya, yb, yc, wts, ctx_len=ctx_len, lat0=lat0, need_ctx=True)
        else:
            out = _merge(x_all, mod4, ng, ya, yb, yc, wts, ctx_len=ctx_len, lat0=lat0, need_ctx=False,
                         final_norm=final_norm.reshape(1, d))
    return out
```

```python
import functools
import math

import jax
import jax.numpy as jnp
from jax import lax
from jax.experimental import pallas as pl
from jax.experimental.pallas import tpu as pltpu

F32 = jnp.float32
BF16 = jnp.bfloat16

GRID_W = 64
N_BRANCH = 3
EPS = 1e-6
ROPE_BASE = 10000.0
NEG_INF = -1e30
LOG2E = math.log2(math.e)

MLA_HEADS = 8
MLA_NOPE = 64
MLA_ROPE = 32
MLA_V = 128
MLA_Q_RANK = 384
MLA_KV_RANK = 256

DIFF_HEADS = 8
DIFF_HD = 64
DIFF_V = 2 * DIFF_HD

SWA_HEADS = 16
SWA_KV_HEADS = 4
SWA_GROUP = SWA_HEADS // SWA_KV_HEADS
SWA_HD = 64
WINDOW = 128

LANE = 128

VMEM_LIMIT = 56 * 1024 * 1024

PROJ_TM = 256
MERGE_TM = 256
FLASH_TQ = 512
FLASH_TK = 512
FLASH_UNROLL = 4
SWA_TQ = 256


def _const_spec(shape):
    zeros = (0,) * len(shape)
    return pl.BlockSpec(shape, lambda *_: zeros, pipeline_mode=pl.Buffered(1))


def _rms(x, eps=EPS):
    return x * lax.rsqrt(jnp.mean(x * x, axis=-1, keepdims=True) + eps)


def _mod_kernel(c_ref, w_ref, b_ref, o_ref):
    cv = c_ref[...]
    s = cv * jax.nn.sigmoid(cv)
    o_ref[0] = jnp.dot(s, w_ref[0], preferred_element_type=F32) + b_ref[0]


def _modulation(c_rows, w_mod, b_mod):
    depth, d, d3 = w_mod.shape
    rows = c_rows.shape[0]
    tn = d
    return pl.pallas_call(
        _mod_kernel,
        grid=(depth, d3 // tn),
        in_specs=[
            pl.BlockSpec((rows, d), lambda l, n: (0, 0)),
            pl.BlockSpec((1, d, tn), lambda l, n: (l, 0, n)),
            pl.BlockSpec((1, 1, tn), lambda l, n: (l, 0, n)),
        ],
        out_specs=pl.BlockSpec((1, rows, tn), lambda l, n: (l, 0, n)),
        out_shape=jax.ShapeDtypeStruct((depth, rows, d3), F32),
        name="modulation",
        compiler_params=pltpu.CompilerParams(
            dimension_semantics=("arbitrary", "arbitrary"), vmem_limit_bytes=VMEM_LIMIT),
    )(c_rows, w_mod, b_mod.reshape(depth, 1, d3))


def _rope(x, cc, ss, lo_mask, half):
    partner = jnp.where(lo_mask, pltpu.roll(x, LANE - half, axis=1), pltpu.roll(x, half, axis=1))
    return x * cc + partner * ss


def _proj_kernel(x_ref, mod_ref, ng_ref, w1_ref, qn_ref, kvn_ref, wuq_ref, wukv_ref, wd_ref, ws_ref,
                 cch_ref, ssh_ref, ccm_ref, ssm_ref,
                 mqt_ref, mk_ref, mvt_ref, dqt_ref, dk_ref, dvt_ref, sqt_ref, sk_ref, svt_ref,
                 *, mla_scale, hd_scale):
    x = x_ref[0]
    shift = mod_ref[0, 0, 0:1, :]
    scale = mod_ref[0, 0, 1:2, :]
    h = (_rms(x) * ng_ref[...] * (1.0 + scale) + shift).astype(BF16)
    tm = x.shape[0]

    lane = lax.broadcasted_iota(jnp.int32, (tm, LANE), 1)
    lo_hd = (lane & (SWA_HD // 2)) == 0
    lo_mla = (lane & (MLA_ROPE // 2)) == 0
    cch, ssh = cch_ref[...], ssh_ref[...]
    ccm, ssm = ccm_ref[...], ssm_ref[...]

    t1 = jnp.dot(h, w1_ref[...], preferred_element_type=F32)
    qc = t1[:, :MLA_Q_RANK]
    kvc = t1[:, MLA_Q_RANK:MLA_Q_RANK + MLA_KV_RANK]
    kr = t1[:, MLA_Q_RANK + MLA_KV_RANK:]
    qn = (_rms(qc) * qn_ref[...]).astype(BF16)
    kvn = (_rms(kvc) * kvn_ref[...]).astype(BF16)
    q_m = jnp.dot(qn, wuq_ref[...], preferred_element_type=F32)
    kv_m = jnp.dot(kvn, wukv_ref[...], preferred_element_type=F32)
    kr_roped = _rope(kr, ccm, ssm, lo_mla, MLA_ROPE // 2)
    nh = MLA_HEADS * LANE
    for hh in range(MLA_HEADS):
        sl = slice(hh * LANE, (hh + 1) * LANE)
        qh = _rope(q_m[:, sl], ccm, ssm, lo_mla, MLA_ROPE // 2) * mla_scale
        mqt_ref[0, hh] = qh.T.astype(BF16)
        mk_ref[0, hh] = (kv_m[:, sl] + kr_roped).astype(BF16)
        mvt_ref[0, hh] = kv_m[:, nh + hh * LANE:nh + (hh + 1) * LANE].T.astype(BF16)

    t2 = jnp.dot(h, wd_ref[...], preferred_element_type=F32)
    nd = DIFF_HEADS * LANE
    for hh in range(DIFF_HEADS):
        sl = slice(hh * LANE, (hh + 1) * LANE)
        qh = _rope(t2[:, sl], cch, ssh, lo_hd, DIFF_HD // 2) * hd_scale
        dqt_ref[0, hh] = qh.T.astype(BF16)
        kh = _rope(t2[:, nd + hh * LANE:nd + (hh + 1) * LANE], cch, ssh, lo_hd, DIFF_HD // 2)
        dk_ref[0, hh] = kh.astype(BF16)
        dvt_ref[0, hh] = t2[:, 2 * nd + hh * LANE:2 * nd + (hh + 1) * LANE].T.astype(BF16)

    t3 = jnp.dot(h, ws_ref[...], preferred_element_type=F32)
    nq = SWA_HEADS * SWA_HD
    nkv = SWA_KV_HEADS * SWA_HD
    for j in range(nq // LANE):
        qt = (_rope(t3[:, j * LANE:(j + 1) * LANE], cch, ssh, lo_hd, SWA_HD // 2) * hd_scale).T.astype(BF16)
        sqt_ref[0, 2 * j] = qt[:SWA_HD]
        sqt_ref[0, 2 * j + 1] = qt[SWA_HD:]
    for j in range(nkv // LANE):
        kh = _rope(t3[:, nq + j * LANE:nq + (j + 1) * LANE], cch, ssh, lo_hd, SWA_HD // 2).astype(BF16)
        sk_ref[0, 2 * j] = kh[:, :SWA_HD]
        sk_ref[0, 2 * j + 1] = kh[:, SWA_HD:]
        vt = t3[:, nq + nkv + j * LANE:nq + nkv + (j + 1) * LANE].T.astype(BF16)
        svt_ref[0, 2 * j] = vt[:SWA_HD]
        svt_ref[0, 2 * j + 1] = vt[SWA_HD:]


def _project(x_all, mod4, norm_g, wts, tabs, lat0):
    b, t, d = x_all.shape
    tm = PROJ_TM
    lat_blk = lat0 // tm
    mla_scale = (MLA_NOPE + MLA_ROPE) ** -0.5 * LOG2E
    hd_scale = SWA_HD ** -0.5 * LOG2E

    def headmajor(nh, natural, width=LANE):
        if natural:
            return (jax.ShapeDtypeStruct((b, nh, t, width), BF16),
                    pl.BlockSpec((1, nh, tm, width), lambda bi, i: (bi, 0, i, 0)))
        return (jax.ShapeDtypeStruct((b, nh, width, t), BF16),
                pl.BlockSpec((1, nh, width, tm), lambda bi, i: (bi, 0, 0, i)))

    outs = [headmajor(MLA_HEADS, False), headmajor(MLA_HEADS, True), headmajor(MLA_HEADS, False),
            headmajor(DIFF_HEADS, False), headmajor(DIFF_HEADS, True), headmajor(DIFF_HEADS, False),
            headmajor(SWA_HEADS, False, SWA_HD), headmajor(SWA_KV_HEADS, True, SWA_HD),
            headmajor(SWA_KV_HEADS, False, SWA_HD)]
    tab_spec = pl.BlockSpec((tm, LANE), lambda bi, i: (i, 0))
    in_specs = [
        pl.BlockSpec((1, tm, d), lambda bi, i: (bi, i, 0)),
        pl.BlockSpec((1, 1, 3, d), lambda bi, i: (bi, jnp.where(i >= lat_blk, 1, 0), 0, 0)),
        _const_spec((1, d)),
        _const_spec(wts["w1"].shape), _const_spec((1, MLA_Q_RANK)), _const_spec((1, MLA_KV_RANK)),
        _const_spec(wts["wuq"].shape), _const_spec(wts["wukv"].shape),
        _const_spec(wts["wd"].shape), _const_spec(wts["ws"].shape),
        tab_spec, tab_spec, tab_spec, tab_spec,
    ]
    return pl.pallas_call(
        functools.partial(_proj_kernel, mla_scale=mla_scale, hd_scale=hd_scale),
        grid=(b, t // tm),
        in_specs=in_specs,
        out_specs=[o[1] for o in outs],
        out_shape=[o[0] for o in outs],
        name="proj",
        compiler_params=pltpu.CompilerParams(
            dimension_semantics=("parallel", "arbitrary"), vmem_limit_bytes=VMEM_LIMIT),
    )(x_all, mod4, norm_g, wts["w1"], wts["qn"], wts["kvn"], wts["wuq"], wts["wukv"], wts["wd"], wts["ws"],
      tabs["cch"], tabs["ssh"], tabs["ccm"], tabs["ssm"])


def _flash_kernel(*refs, n_sub, tq, tk, ctx_len, lat0, n_chunks, unroll, first_blk, lam_scale):
    if n_sub == 2:
        lam_ref, qt_ref, k_ref, vt_ref, subln_ref, o_ref, acc_ref, m_ref, l_ref, s_ref = refs
    else:
        qt_ref, k_ref, vt_ref, o_ref, acc_ref, m_ref, l_ref, s_ref = refs
    qt = qt_ref[0, 0]
    if n_sub == 2:
        row = lax.broadcasted_iota(jnp.int32, qt.shape, 0)
        zero = jnp.zeros_like(qt)
        qt = jnp.concatenate([jnp.where(row < DIFF_HD, qt, zero), jnp.where(row >= DIFF_HD, qt, zero)], axis=1)

    def scores(start, size):
        kc = k_ref[0, 0, pl.ds(start, size), :]
        return jnp.dot(kc, qt, preferred_element_type=F32)

    def pv(start, size, p):
        vc = vt_ref[0, 0, :, pl.ds(start, size)]
        return jnp.dot(vc, p.astype(BF16), preferred_element_type=F32)

    def chunk_start(c):
        return pl.multiple_of(lat0 + c * tk, tk)

    s = scores(0, ctx_len)
    s_ref[0] = scores(chunk_start(0), tk)
    m0 = jnp.max(s, axis=0, keepdims=True)
    p = jnp.exp2(s - m0)
    m_ref[...] = m0
    l_ref[...] = jnp.sum(p, axis=0, keepdims=True)
    acc_ref[...] = pv(0, ctx_len, p)

    def process(s, m, l, c):
        m_new = jnp.maximum(m, jnp.max(s, axis=0, keepdims=True))
        p = jnp.exp2(s - m_new)
        alpha = jnp.exp2(m - m_new)
        l_new = alpha * l + jnp.sum(p, axis=0, keepdims=True)
        acc_ref[...] = alpha * acc_ref[...] + pv(chunk_start(c), tk, p)
        return m_new, l_new

    def group(g, m, l, last):
        c0 = g * unroll
        for j in range(unroll):
            if not (last and j == unroll - 1):
                s_ref[(j + 1) % 2] = scores(chunk_start(c0 + j + 1), tk)
            m, l = process(s_ref[j % 2], m, l, c0 + j)
        return m, l

    @pl.when(pl.program_id(2) + first_blk > 0)
    def _():
        n_groups = n_chunks // unroll
        m, l = lax.fori_loop(0, n_groups - 1, lambda g, ml: group(g, ml[0], ml[1], False),
                             (m_ref[...], l_ref[...]))
        m, l = group(n_groups - 1, m, l, True)
        l_ref[...] = l

    o = acc_ref[...] * (1.0 / l_ref[...])
    if n_sub == 2:
        lam = lam_ref[0]
        d = o[:, :tq] - lam * o[:, tq:]
        y = d * lax.rsqrt(jnp.mean(d * d, axis=0, keepdims=True) + EPS) * subln_ref[...] * lam_scale
    else:
        y = o
    o_ref[0] = y.T.astype(o_ref.dtype)


def _flash(qt, k, vt, *, n_sub, ctx_len, lat0, need_ctx, lam=None, subln=None, lam_scale=1.0):
    b, nh, _, t = qt.shape
    tq, tk = FLASH_TQ, FLASH_TK
    n_chunks = (t - lat0) // tk
    first_blk = 0 if need_ctx else lat0 // tq
    nq = t // tq - first_blk
    in_specs = [
        pl.BlockSpec((1, 1, LANE, tq), lambda bi, hi, i: (bi, hi, 0, i + first_blk)),
        pl.BlockSpec((1, 1, t, LANE), lambda bi, hi, i: (bi, hi, 0, 0)),
        pl.BlockSpec((1, 1, LANE, t), lambda bi, hi, i: (bi, hi, 0, 0)),
    ]
    args = [qt, k, vt]
    if n_sub == 2:
        in_specs = [pl.BlockSpec(memory_space=pltpu.SMEM)] + in_specs + [_const_spec((LANE, 1))]
        args = [lam] + args + [subln]
    return pl.pallas_call(
        functools.partial(_flash_kernel, n_sub=n_sub, tq=tq, tk=tk, ctx_len=ctx_len, lat0=lat0,
                          n_chunks=n_chunks, unroll=math.gcd(n_chunks, FLASH_UNROLL), first_blk=first_blk,
                          lam_scale=lam_scale),
        grid=(b, nh, nq),
        in_specs=in_specs,
        out_specs=pl.BlockSpec((1, tq, LANE), lambda bi, hi, i: (bi, i + first_blk, hi)),
        out_shape=jax.ShapeDtypeStruct((b, t, nh * LANE), BF16),
        scratch_shapes=[pltpu.VMEM((LANE, n_sub * tq), F32), pltpu.VMEM((1, n_sub * tq), F32),
                        pltpu.VMEM((1, n_sub * tq), F32), pltpu.VMEM((2, tk, n_sub * tq), F32)],
        name="flash_diff" if n_sub == 2 else "flash_mla",
        compiler_params=pltpu.CompilerParams(
            dimension_semantics=("parallel", "parallel", "arbitrary"), vmem_limit_bytes=VMEM_LIMIT),
    )(*args)


def _swa_kernel(sink_ref, qt_ref, k_ref, vt_ref, o_ref, *, tq, ctx_len, lat0, t_total, blk_of):
    g = pl.program_id(1)
    qs = blk_of(pl.program_id(2)) * tq
    span = tq + 2 * WINDOW
    ws = pl.multiple_of(jnp.clip(qs - WINDOW, 0, t_total - span), LANE)

    qt = jnp.concatenate([qt_ref[0, j] for j in range(SWA_GROUP)], axis=1)
    s_ctx = jnp.dot(k_ref[0, 0, 0:ctx_len, :], qt, preferred_element_type=F32)
    s_loc = jnp.dot(k_ref[0, 0, pl.ds(ws, span), :], qt, preferred_element_type=F32)

    kpos = ws - lat0 + lax.broadcasted_iota(jnp.int32, (span, tq), 0)
    qpos = qs - lat0 + lax.broadcasted_iota(jnp.int32, (span, tq), 1)
    mask = (jnp.abs(qpos - kpos) <= WINDOW) & (kpos >= 0)
    mask = jnp.concatenate([mask] * SWA_GROUP, axis=1)
    s_loc = jnp.where(mask, s_loc, NEG_INF)

    sink = jnp.concatenate(
        [jnp.full((1, tq), sink_ref[g * SWA_GROUP + j] * LOG2E, F32) for j in range(SWA_GROUP)], axis=1)
    m = jnp.maximum(jnp.maximum(jnp.max(s_ctx, axis=0, keepdims=True), jnp.max(s_loc, axis=0, keepdims=True)), sink)
    p_ctx = jnp.exp2(s_ctx - m)
    p_loc = jnp.exp2(s_loc - m)
    l = jnp.sum(p_ctx, axis=0, keepdims=True) + jnp.sum(p_loc, axis=0, keepdims=True) + jnp.exp2(sink - m)
    ot = (jnp.dot(vt_ref[0, 0, :, 0:ctx_len], p_ctx.astype(BF16), preferred_element_type=F32)
          + jnp.dot(vt_ref[0, 0, :, pl.ds(ws, span)], p_loc.astype(BF16), preferred_element_type=F32))
    ot = ot / l
    o_ref[0] = jnp.concatenate([ot[:, j * tq:(j + 1) * tq].T for j in range(SWA_GROUP)], axis=1).astype(o_ref.dtype)


def _swa(sink, qt, k, vt, *, ctx_len, lat0, need_ctx):
    b, _, _, t = qt.shape
    tq = SWA_TQ
    lat_blk = lat0 // tq
    if need_ctx:
        nq = (t - lat0 + ctx_len) // tq
        n_ctx_blk = ctx_len // tq
        blk_of = lambda i: jnp.where(i < n_ctx_blk, i, i + (lat_blk - n_ctx_blk))
    else:
        nq = (t - lat0) // tq
        blk_of = lambda i: i + lat_blk
    return pl.pallas_call(
        functools.partial(_swa_kernel, tq=tq, ctx_len=ctx_len, lat0=lat0, t_total=t, blk_of=blk_of),
        grid=(b, SWA_KV_HEADS, nq),
        in_specs=[
            pl.BlockSpec(memory_space=pltpu.SMEM),
            pl.BlockSpec((1, SWA_GROUP, SWA_HD, tq), lambda bi, gi, i: (bi, gi, 0, blk_of(i))),
            pl.BlockSpec((1, 1, t, SWA_HD), lambda bi, gi, i: (bi, gi, 0, 0)),
            pl.BlockSpec((1, 1, SWA_HD, t), lambda bi, gi, i: (bi, gi, 0, 0)),
        ],
        out_specs=pl.BlockSpec((1, tq, SWA_GROUP * SWA_HD), lambda bi, gi, i: (bi, blk_of(i), gi)),
        out_shape=jax.ShapeDtypeStruct((b, t, SWA_HEADS * SWA_HD), BF16),
        name="swa",
        compiler_params=pltpu.CompilerParams(
            dimension_semantics=("parallel", "parallel", "arbitrary"), vmem_limit_bytes=VMEM_LIMIT),
    )(sink, qt, k, vt)


def _merge_kernel(*refs, final):
    if final:
        x_ref, mod_ref, ng_ref, ya_ref, yb_ref, yc_ref, wz_ref, wg_ref, wb_ref, wo_ref, fn_ref, o_ref = refs
    else:
        x_ref, mod_ref, ng_ref, ya_ref, yb_ref, yc_ref, wz_ref, wg_ref, wb_ref, wo_ref, o_ref = refs
    x = x_ref[0]
    d = x.shape[1]
    shift = mod_ref[0, 0, 0:1, :]
    scale = mod_ref[0, 0, 1:2, :]
    gate = mod_ref[0, 0, 2:3, :]
    h = (_rms(x) * ng_ref[...] * (1.0 + scale) + shift).astype(BF16)
    acc = jnp.zeros(x.shape, F32)
    for r, y_ref in enumerate((ya_ref, yb_ref, yc_ref)):
        w = y_ref.shape[2]
        z = jnp.dot(h, wz_ref[:, r * w:(r + 1) * w], preferred_element_type=F32)
        u = (y_ref[0].astype(F32) * (z * jax.nn.sigmoid(z))).astype(BF16)
        tr = jnp.dot(u, wb_ref[r], preferred_element_type=F32)
        gm = jnp.dot(h, wg_ref[:, r * d:(r + 1) * d], preferred_element_type=F32)
        acc = acc + jax.nn.sigmoid(gm) * tr
    out = jnp.dot(acc.astype(BF16), wo_ref[...], preferred_element_type=F32)
    xn = x + gate * out
    if final:
        xn = _rms(xn) * fn_ref[...]
    o_ref[0] = xn


def _merge(x_all, mod4, norm_g, ya, yb, yc, wts, *, ctx_len, lat0, need_ctx, final_norm=None):
    b, t, d = x_all.shape
    tm = MERGE_TM
    lat_blk = lat0 // tm
    final = final_norm is not None
    if need_ctx:
        n_ctx_blk = ctx_len // tm
        nblk = (t - lat0 + ctx_len) // tm
        blk_of = lambda i: jnp.where(i < n_ctx_blk, i, i + (lat_blk - n_ctx_blk))
    else:
        nblk = (t - lat0) // tm
        blk_of = lambda i: i + lat_blk
    row_spec = pl.BlockSpec((1, tm, d), lambda bi, i: (bi, blk_of(i), 0))
    in_specs = [
        row_spec,
        pl.BlockSpec((1, 1, 3, d), lambda bi, i: (bi, jnp.where(blk_of(i) >= lat_blk, 1, 0), 0, 0)),
        _const_spec((1, d)),
        row_spec, row_spec, row_spec,
        _const_spec(wts["wz"].shape), _const_spec(wts["wg"].shape),
        _const_spec(wts["wb"].shape), _const_spec(wts["wo"].shape),
    ]
    args = [x_all, mod4, norm_g, ya, yb, yc, wts["wz"], wts["wg"], wts["wb"], wts["wo"]]
    if final:
        in_specs.append(_const_spec((1, d)))
        args.append(final_norm)
        out_shape = jax.ShapeDtypeStruct((b, t - lat0, d), F32)
        out_spec = pl.BlockSpec((1, tm, d), lambda bi, i: (bi, i, 0))
        aliases = {}
    else:
        out_shape = jax.ShapeDtypeStruct((b, t, d), F32)
        out_spec = row_spec
        aliases = {0: 0}
    return pl.pallas_call(
        functools.partial(_merge_kernel, final=final),
        grid=(b, nblk),
        in_specs=in_specs,
        out_specs=out_spec,
        out_shape=out_shape,
        input_output_aliases=aliases,
        name="merge",
        compiler_params=pltpu.CompilerParams(
            dimension_semantics=("parallel", "arbitrary"), vmem_limit_bytes=VMEM_LIMIT),
    )(*args)


def _layer_weights(l, w_in, mla_q_norm, mla_w_uq, mla_kv_norm, mla_w_ukv, w_branch, w_out):
    d = w_in.shape[1]
    splits = (MLA_Q_RANK, MLA_KV_RANK, MLA_ROPE,
              DIFF_HEADS * 2 * DIFF_HD, DIFF_HEADS * 2 * DIFF_HD, DIFF_HEADS * DIFF_V,
              SWA_HEADS * SWA_HD, SWA_KV_HEADS * SWA_HD, SWA_KV_HEADS * SWA_HD,
              N_BRANCH * w_branch.shape[2], N_BRANCH * d)
    offs = [0]
    for s in splits:
        offs.append(offs[-1] + s)
    w = w_in[l]
    col = lambda a, b_: w[:, offs[a]:offs[b_]]
    zeros = lambda n: jnp.zeros((d, n), w.dtype)
    w1 = jnp.concatenate([col(0, 2), zeros(MLA_NOPE), col(2, 3), zeros(LANE - MLA_NOPE - MLA_ROPE)], axis=1)
    qk = MLA_NOPE + MLA_ROPE
    wuq = jnp.pad(mla_w_uq[l].reshape(MLA_Q_RANK, MLA_HEADS, qk), ((0, 0), (0, 0), (0, LANE - qk)))
    wukv = mla_w_ukv[l].reshape(MLA_KV_RANK, MLA_HEADS, MLA_NOPE + MLA_V)
    wuk = jnp.pad(wukv[:, :, :MLA_NOPE], ((0, 0), (0, 0), (0, LANE - MLA_NOPE)))
    wuv = wukv[:, :, MLA_NOPE:]
    return {
        "w1": w1.astype(BF16),
        "qn": mla_q_norm[l].reshape(1, -1),
        "kvn": mla_kv_norm[l].reshape(1, -1),
        "wuq": wuq.reshape(MLA_Q_RANK, MLA_HEADS * LANE).astype(BF16),
        "wukv": jnp.concatenate([wuk.reshape(MLA_KV_RANK, -1), wuv.reshape(MLA_KV_RANK, -1)], axis=1).astype(BF16),
        "wd": col(3, 6).astype(BF16),
        "ws": col(6, 9).astype(BF16),
        "wz": col(9, 10).astype(BF16),
        "wg": col(10, 11).astype(BF16),
        "wb": w_branch[l].astype(BF16),
        "wo": w_out[l].astype(BF16),
    }


def _rope_tables(n_lat, lat0):
    rows = n_lat // GRID_W
    row = jnp.repeat(jnp.arange(rows), GRID_W).astype(F32)
    col = jnp.tile(jnp.arange(GRID_W), rows).astype(F32)

    def cs(rot_dim):
        n_freq = rot_dim // 4
        inv_freq = ROPE_BASE ** (-jnp.arange(n_freq, dtype=F32) / n_freq)
        ang = jnp.concatenate([row[:, None] * inv_freq, col[:, None] * inv_freq], axis=-1)
        cos = jnp.concatenate([jnp.ones((lat0, rot_dim // 2), F32), jnp.cos(ang)], axis=0)
        sin = jnp.concatenate([jnp.zeros((lat0, rot_dim // 2), F32), jnp.sin(ang)], axis=0)
        return cos, sin

    cos, sin = cs(SWA_HD)
    cch = jnp.tile(cos, (1, LANE // (SWA_HD // 2)))
    ssh = jnp.tile(jnp.concatenate([-sin, sin], axis=1), (1, LANE // SWA_HD))
    cos, sin = cs(MLA_ROPE)
    t = cos.shape[0]
    pad = LANE - MLA_NOPE - MLA_ROPE
    ccm = jnp.concatenate([jnp.ones((t, MLA_NOPE), F32), cos, cos, jnp.zeros((t, pad), F32)], axis=1)
    ssm = jnp.concatenate([jnp.zeros((t, MLA_NOPE), F32), -sin, sin, jnp.zeros((t, pad), F32)], axis=1)
    return {"cch": cch, "ssh": ssh, "ccm": ccm, "ssm": ssm}


def kernel(x, c, ctx, c_ctx, w_mod, b_mod, norm_g, w_in, mla_q_norm, mla_w_uq, mla_kv_norm, mla_w_ukv,
           diff_lq1, diff_lk1, diff_lq2, diff_lk2, diff_subln, swa_sink, w_branch, w_out, final_norm):
    b, n_lat, d = x.shape
    ctx_len = ctx.shape[1]
    depth = w_mod.shape[0]
    lat0 = -(-ctx_len // FLASH_TQ) * FLASH_TQ
    assert ctx_len % SWA_TQ == 0 and ctx_len % MERGE_TM == 0 and n_lat % FLASH_TK == 0 and n_lat % GRID_W == 0

    x_all = jnp.concatenate([ctx, jnp.zeros((b, lat0 - ctx_len, d), x.dtype), x], axis=1)
    tabs = _rope_tables(n_lat, lat0)

    c_rows = jnp.concatenate([c, c_ctx[None, :], jnp.zeros((16 - b - 1, d), c.dtype)], axis=0)
    mod = _modulation(c_rows, w_mod, b_mod)

    out = None
    for l in range(depth):
        need_ctx = l < depth - 1
        mod_lat = mod[l, :b].reshape(b, 1, 3, d)
        mod_ctx = jnp.broadcast_to(mod[l, b].reshape(1, 1, 3, d), (b, 1, 3, d))
        mod4 = jnp.concatenate([mod_ctx, mod_lat], axis=1)
        wts = _layer_weights(l, w_in, mla_q_norm, mla_w_uq, mla_kv_norm, mla_w_ukv, w_branch, w_out)
        ng = norm_g[l].reshape(1, d)

        mqt, mk, mvt, dqt, dk, dvt, sqt, sk, svt = _project(x_all, mod4, ng, wts, tabs, lat0)

        lam_init = 0.8 - 0.6 * math.exp(-0.3 * l)
        lam = (jnp.exp(jnp.sum(diff_lq1[l].astype(F32) * diff_lk1[l].astype(F32)))
               - jnp.exp(jnp.sum(diff_lq2[l].astype(F32) * diff_lk2[l].astype(F32))) + lam_init).reshape(1)

        ya = _flash(mqt, mk, mvt, n_sub=1, ctx_len=ctx_len, lat0=lat0, need_ctx=need_ctx)
        yb = _flash(dqt, dk, dvt, n_sub=2, ctx_len=ctx_len, lat0=lat0, need_ctx=need_ctx,
                    lam=lam, subln=diff_subln[l].reshape(LANE, 1), lam_scale=1.0 - lam_init)
        yc = _swa(swa_sink[l], sqt, sk, svt, ctx_len=ctx_len, lat0=lat0, need_ctx=need_ctx)

        if need_ctx:
            x_all = _merge(x_all, mod4, ng, ya, yb, yc, wts, ctx_len=ctx_len, lat0=lat0, need_ctx=True)
        else:
            out = _merge(x_all, mod4, ng, ya, yb, yc, wts, ctx_len=ctx_len, lat0=lat0, need_ctx=False,
                         final_norm=final_norm.reshape(1, d))
    return out
```

```python
import functools
import math

import jax
import jax.numpy as jnp
from jax import lax
from jax.experimental import pallas as pl
from jax.experimental.pallas import tpu as pltpu

F32 = jnp.float32
BF16 = jnp.bfloat16

GRID_W = 64
N_BRANCH = 3
EPS = 1e-6
ROPE_BASE = 10000.0
NEG_INF = -1e30
LOG2E = math.log2(math.e)

MLA_HEADS = 8
MLA_NOPE = 64
MLA_ROPE = 32
MLA_V = 128
MLA_Q_RANK = 384
MLA_KV_RANK = 256

DIFF_HEADS = 8
DIFF_HD = 64
DIFF_V = 2 * DIFF_HD

SWA_HEADS = 16
SWA_KV_HEADS = 4
SWA_GROUP = SWA_HEADS // SWA_KV_HEADS
SWA_HD = 64
WINDOW = 128

LANE = 128
MXU_DIM = 256

VMEM_LIMIT = 56 * 1024 * 1024

ROW_TM = 256
FLASH_TQ_MLA = 1024
FLASH_TQ_DIFF = 512
FLASH_TK_MAX = 768
FLASH_UNROLL = 4


def _const_spec(shape):
    zeros = (0,) * len(shape)
    return pl.BlockSpec(shape, lambda *_: zeros, pipeline_mode=pl.Buffered(1))


def _rms(x, eps=EPS):
    return x * lax.rsqrt(jnp.mean(x * x, axis=-1, keepdims=True) + eps)


def _mod_kernel(c_ref, w_ref, b_ref, o_ref):
    cv = c_ref[...]
    s = cv * jax.nn.sigmoid(cv)
    o_ref[0] = jnp.dot(s, w_ref[0], preferred_element_type=F32) + b_ref[0]


def _modulation(c_rows, w_mod, b_mod):
    depth, d, d3 = w_mod.shape
    rows = c_rows.shape[0]
    tn = d
    return pl.pallas_call(
        _mod_kernel,
        grid=(depth, d3 // tn),
        in_specs=[
            pl.BlockSpec((rows, d), lambda l, n: (0, 0)),
            pl.BlockSpec((1, d, tn), lambda l, n: (l, 0, n)),
            pl.BlockSpec((1, 1, tn), lambda l, n: (l, 0, n)),
        ],
        out_specs=pl.BlockSpec((1, rows, tn), lambda l, n: (l, 0, n)),
        out_shape=jax.ShapeDtypeStruct((depth, rows, d3), F32),
        name="modulation",
        compiler_params=pltpu.CompilerParams(
            dimension_semantics=("arbitrary", "arbitrary"), vmem_limit_bytes=VMEM_LIMIT),
    )(c_rows, w_mod, b_mod.reshape(depth, 1, d3))


def _rope(x, cc, ss, lo_mask, half):
    partner = jnp.where(lo_mask, pltpu.roll(x, LANE - half, axis=1), pltpu.roll(x, half, axis=1))
    return x * cc + partner * ss


def _proj_kernel(x_ref, mod_ref, ng_ref, w1_ref, qn_ref, kvn_ref, wuq_ref, wukv_ref, wd_ref, ws_ref,
                 cch_ref, ssh_ref, ccm_ref, ssm_ref,
                 mqt_ref, mk_ref, mvt_ref, dqt_ref, dk_ref, dvt_ref, sqt_ref, sk_ref, svt_ref,
                 *, mla_scale, hd_scale):
    x = x_ref[0]
    shift = mod_ref[0, 0, 0:1, :]
    scale = mod_ref[0, 0, 1:2, :]
    h = (_rms(x) * ng_ref[...] * (1.0 + scale) + shift).astype(BF16)
    tm = x.shape[0]

    lane = lax.broadcasted_iota(jnp.int32, (tm, LANE), 1)
    lo_hd = (lane & (SWA_HD // 2)) == 0
    lo_mla = (lane & (MLA_ROPE // 2)) == 0
    cch, ssh = cch_ref[...], ssh_ref[...]
    ccm, ssm = ccm_ref[...], ssm_ref[...]

    t1 = jnp.dot(h, w1_ref[...], preferred_element_type=F32)
    qc = t1[:, :MLA_Q_RANK]
    kvc = t1[:, MLA_Q_RANK:MLA_Q_RANK + MLA_KV_RANK]
    kr = t1[:, MLA_Q_RANK + MLA_KV_RANK:]
    qn = (_rms(qc) * qn_ref[...]).astype(BF16)
    kvn = (_rms(kvc) * kvn_ref[...]).astype(BF16)
    q_m = jnp.dot(qn, wuq_ref[...], preferred_element_type=F32)
    kv_m = jnp.dot(kvn, wukv_ref[...], preferred_element_type=F32)
    kr_roped = _rope(kr, ccm, ssm, lo_mla, MLA_ROPE // 2)
    nh = MLA_HEADS * LANE
    for hh in range(MLA_HEADS):
        sl = slice(hh * LANE, (hh + 1) * LANE)
        qh = _rope(q_m[:, sl], ccm, ssm, lo_mla, MLA_ROPE // 2) * mla_scale
        mqt_ref[0, hh] = qh.T.astype(BF16)
        mk_ref[0, hh] = (kv_m[:, sl] + kr_roped).astype(BF16)
        mvt_ref[0, hh] = kv_m[:, nh + hh * LANE:nh + (hh + 1) * LANE].T.astype(BF16)

    t2 = jnp.dot(h, wd_ref[...], preferred_element_type=F32)
    nd = DIFF_HEADS * LANE
    for hh in range(DIFF_HEADS):
        sl = slice(hh * LANE, (hh + 1) * LANE)
        qh = _rope(t2[:, sl], cch, ssh, lo_hd, DIFF_HD // 2) * hd_scale
        dqt_ref[0, hh] = qh.T.astype(BF16)
        kh = _rope(t2[:, nd + hh * LANE:nd + (hh + 1) * LANE], cch, ssh, lo_hd, DIFF_HD // 2)
        dk_ref[0, hh] = kh.astype(BF16)
        dvt_ref[0, hh] = t2[:, 2 * nd + hh * LANE:2 * nd + (hh + 1) * LANE].T.astype(BF16)

    t3 = jnp.dot(h, ws_ref[...], preferred_element_type=F32)
    nq = SWA_HEADS * SWA_HD
    nkv = SWA_KV_HEADS * SWA_HD
    for j in range(nq // LANE):
        qt = (_rope(t3[:, j * LANE:(j + 1) * LANE], cch, ssh, lo_hd, SWA_HD // 2) * hd_scale).T.astype(BF16)
        sqt_ref[0, 2 * j] = qt[:SWA_HD]
        sqt_ref[0, 2 * j + 1] = qt[SWA_HD:]
    for j in range(nkv // LANE):
        kh = _rope(t3[:, nq + j * LANE:nq + (j + 1) * LANE], cch, ssh, lo_hd, SWA_HD // 2).astype(BF16)
        sk_ref[0, 2 * j] = kh[:, :SWA_HD]
        sk_ref[0, 2 * j + 1] = kh[:, SWA_HD:]
        vt = t3[:, nq + nkv + j * LANE:nq + nkv + (j + 1) * LANE].T.astype(BF16)
        svt_ref[0, 2 * j] = vt[:SWA_HD]
        svt_ref[0, 2 * j + 1] = vt[SWA_HD:]


def _mod_spec(d, lat_blk):
    return pl.BlockSpec((1, 1, 3, d), lambda bi, i: (bi, jnp.where(i < lat_blk, 1, 0), 0, 0))


def _project(x_all, mod4, norm_g, wts, tabs, n_lat):
    b, t, d = x_all.shape
    tm = ROW_TM
    mla_scale = (MLA_NOPE + MLA_ROPE) ** -0.5 * LOG2E
    hd_scale = SWA_HD ** -0.5 * LOG2E

    def headmajor(nh, natural, width=LANE):
        if natural:
            return (jax.ShapeDtypeStruct((b, nh, t, width), BF16),
                    pl.BlockSpec((1, nh, tm, width), lambda bi, i: (bi, 0, i, 0)))
        return (jax.ShapeDtypeStruct((b, nh, width, t), BF16),
                pl.BlockSpec((1, nh, width, tm), lambda bi, i: (bi, 0, 0, i)))

    outs = [headmajor(MLA_HEADS, False), headmajor(MLA_HEADS, True), headmajor(MLA_HEADS, False),
            headmajor(DIFF_HEADS, False), headmajor(DIFF_HEADS, True), headmajor(DIFF_HEADS, False),
            headmajor(SWA_HEADS, False, SWA_HD), headmajor(SWA_KV_HEADS, True, SWA_HD),
            headmajor(SWA_KV_HEADS, False, SWA_HD)]
    tab_spec = pl.BlockSpec((tm, LANE), lambda bi, i: (i, 0))
    in_specs = [
        pl.BlockSpec((1, tm, d), lambda bi, i: (bi, i, 0)),
        _mod_spec(d, n_lat // tm),
        _const_spec((1, d)),
        _const_spec(wts["w1"].shape), _const_spec((1, MLA_Q_RANK)), _const_spec((1, MLA_KV_RANK)),
        _const_spec(wts["wuq"].shape), _const_spec(wts["wukv"].shape),
        _const_spec(wts["wd"].shape), _const_spec(wts["ws"].shape),
        tab_spec, tab_spec, tab_spec, tab_spec,
    ]
    return pl.pallas_call(
        functools.partial(_proj_kernel, mla_scale=mla_scale, hd_scale=hd_scale),
        grid=(b, t // tm),
        in_specs=in_specs,
        out_specs=[o[1] for o in outs],
        out_shape=[o[0] for o in outs],
        name="proj",
        compiler_params=pltpu.CompilerParams(
            dimension_semantics=("parallel", "arbitrary"), vmem_limit_bytes=VMEM_LIMIT),
    )(x_all, mod4, norm_g, wts["w1"], wts["qn"], wts["kvn"], wts["wuq"], wts["wukv"], wts["wd"], wts["ws"],
      tabs["cch"], tabs["ssh"], tabs["ccm"], tabs["ssm"])


def _stack_sub_heads(qt, n_sub):
    if n_sub == 1:
        return qt
    row = lax.broadcasted_iota(jnp.int32, qt.shape, 0)
    zero = jnp.zeros_like(qt)
    return jnp.concatenate([jnp.where(row < DIFF_HD, qt, zero), jnp.where(row >= DIFF_HD, qt, zero)], axis=1)


def _flash_finalize(o, o_ref, n_sub, tq, lam_ref, subln_ref, lam_scale):
    if n_sub == 2:
        lam = lam_ref[0]
        d = o[:, :tq] - lam * o[:, tq:]
        y = d * lax.rsqrt(jnp.mean(d * d, axis=0, keepdims=True) + EPS) * subln_ref[...] * lam_scale
    else:
        y = o
    o_ref[0] = y.T.astype(o_ref.dtype)


def _flash_kernel(*refs, n_sub, tq, tk, n_chunks, unroll, lam_scale):
    if n_sub == 2:
        lam_ref, qt_ref, k_ref, vt_ref, subln_ref, o_ref, acc_ref, s_ref = refs
    else:
        qt_ref, k_ref, vt_ref, o_ref, acc_ref, s_ref = refs
        lam_ref = subln_ref = None
    qt = _stack_sub_heads(qt_ref[0, 0], n_sub)
    r = qt.shape[1]

    def chunk_start(c):
        return pl.multiple_of(c * tk, LANE)

    def scores(c):
        kc = k_ref[0, 0, pl.ds(chunk_start(c), tk), :]
        return jnp.dot(kc, qt, preferred_element_type=F32)

    def process(s, m, l, c):
        m_new = jnp.maximum(m, jnp.max(s, axis=0, keepdims=True))
        p = jnp.exp2(s - m_new)
        alpha = jnp.exp2(m - m_new)
        l_new = alpha * l + jnp.sum(p, axis=0, keepdims=True)
        vc = vt_ref[0, 0, :, pl.ds(chunk_start(c), tk)]
        acc_ref[...] = alpha * acc_ref[...] + jnp.dot(vc, p.astype(BF16), preferred_element_type=F32)
        return m_new, l_new

    def run(c0, count, m, l, prefetch_after):
        for j in range(count):
            if j < count - 1 or prefetch_after:
                s_ref[(j + 1) % 2] = scores(c0 + j + 1)
            m, l = process(s_ref[j % 2], m, l, c0 + j)
        return m, l

    s_ref[0] = scores(0)
    acc_ref[...] = jnp.zeros_like(acc_ref)
    m = jnp.full((1, r), NEG_INF, F32)
    l = jnp.zeros((1, r), F32)
    n_loop = (n_chunks - 1) // unroll
    m, l = lax.fori_loop(0, n_loop, lambda g, ml: run(g * unroll, unroll, ml[0], ml[1], True), (m, l))
    m, l = run(n_loop * unroll, n_chunks - n_loop * unroll, m, l, False)

    _flash_finalize(acc_ref[...] * (1.0 / l), o_ref, n_sub, tq, lam_ref, subln_ref, lam_scale)


def _flash_ctx_kernel(*refs, n_sub, tq, lam_scale):
    if n_sub == 2:
        lam_ref, qt_ref, k_ref, vt_ref, subln_ref, _, o_ref = refs
    else:
        qt_ref, k_ref, vt_ref, _, o_ref = refs
        lam_ref = subln_ref = None
    qt = _stack_sub_heads(qt_ref[0, 0], n_sub)
    s = jnp.dot(k_ref[0, 0], qt, preferred_element_type=F32)
    p = jnp.exp2(s - jnp.max(s, axis=0, keepdims=True))
    l = jnp.sum(p, axis=0, keepdims=True)
    o = jnp.dot(vt_ref[0, 0], p.astype(BF16), preferred_element_type=F32) * (1.0 / l)
    _flash_finalize(o, o_ref, n_sub, tq, lam_ref, subln_ref, lam_scale)


def _chunk_size(t):
    return max(k for k in range(MXU_DIM, FLASH_TK_MAX + 1, MXU_DIM) if t % k == 0)


def _flash(qt, k, vt, *, n_sub, n_lat, need_ctx, lam=None, subln=None, lam_scale=1.0):
    b, nh, _, t = qt.shape
    ctx_len = t - n_lat
    tq = FLASH_TQ_DIFF if n_sub == 2 else FLASH_TQ_MLA
    tq = math.gcd(tq, n_lat)
    tk = _chunk_size(t)
    params = pltpu.CompilerParams(
        dimension_semantics=("parallel", "parallel", "arbitrary"), vmem_limit_bytes=VMEM_LIMIT)
    extra_in = [pl.BlockSpec(memory_space=pltpu.SMEM)] if n_sub == 2 else []
    extra_out = [_const_spec((LANE, 1))] if n_sub == 2 else []
    pre = [lam] if n_sub == 2 else []
    post = [subln] if n_sub == 2 else []
    name = "flash_diff" if n_sub == 2 else "flash_mla"

    y = pl.pallas_call(
        functools.partial(_flash_kernel, n_sub=n_sub, tq=tq, tk=tk, n_chunks=t // tk,
                          unroll=FLASH_UNROLL, lam_scale=lam_scale),
        grid=(b, nh, n_lat // tq),
        in_specs=extra_in + [
            pl.BlockSpec((1, 1, LANE, tq), lambda bi, hi, i: (bi, hi, 0, i)),
            pl.BlockSpec((1, 1, t, LANE), lambda bi, hi, i: (bi, hi, 0, 0)),
            pl.BlockSpec((1, 1, LANE, t), lambda bi, hi, i: (bi, hi, 0, 0)),
        ] + extra_out,
        out_specs=pl.BlockSpec((1, tq, LANE), lambda bi, hi, i: (bi, i, hi)),
        out_shape=jax.ShapeDtypeStruct((b, t, nh * LANE), BF16),
        scratch_shapes=[pltpu.VMEM((LANE, n_sub * tq), F32), pltpu.VMEM((2, tk, n_sub * tq), F32)],
        name=name,
        compiler_params=params,
    )(*pre, qt, k, vt, *post)
    if not need_ctx:
        return y

    cblk = n_lat // ctx_len
    return pl.pallas_call(
        functools.partial(_flash_ctx_kernel, n_sub=n_sub, tq=ctx_len, lam_scale=lam_scale),
        grid=(b, nh),
        in_specs=extra_in + [
            pl.BlockSpec((1, 1, LANE, ctx_len), lambda bi, hi: (bi, hi, 0, cblk)),
            pl.BlockSpec((1, 1, ctx_len, LANE), lambda bi, hi: (bi, hi, cblk, 0)),
            pl.BlockSpec((1, 1, LANE, ctx_len), lambda bi, hi: (bi, hi, 0, cblk)),
        ] + extra_out + [pl.BlockSpec(memory_space=pl.ANY)],
        out_specs=pl.BlockSpec((1, ctx_len, LANE), lambda bi, hi: (bi, cblk, hi)),
        out_shape=jax.ShapeDtypeStruct(y.shape, y.dtype),
        input_output_aliases={len(pre) + 3 + len(post): 0},
        name=name + "_ctx",
        compiler_params=pltpu.CompilerParams(
            dimension_semantics=("parallel", "arbitrary"), vmem_limit_bytes=VMEM_LIMIT),
    )(*pre, qt, k, vt, *post, y)


def _swa_kernel(sink_ref, qt_ref, k_ref, vt_ref, o_ref, *, tq, n_lat, ctx_len):
    g = pl.program_id(1)
    qs = pl.program_id(2) * tq
    span = tq + 2 * WINDOW
    ws = pl.multiple_of(jnp.clip(qs - WINDOW, 0, n_lat - span), LANE)

    qt = jnp.concatenate([qt_ref[0, j] for j in range(SWA_GROUP)], axis=1)
    s_ctx = jnp.dot(k_ref[0, 0, n_lat:n_lat + ctx_len, :], qt, preferred_element_type=F32)
    s_loc = jnp.dot(k_ref[0, 0, pl.ds(ws, span), :], qt, preferred_element_type=F32)

    kpos = ws + lax.broadcasted_iota(jnp.int32, (span, tq), 0)
    qpos = qs + lax.broadcasted_iota(jnp.int32, (span, tq), 1)
    mask = (jnp.abs(qpos - kpos) <= WINDOW) & (qpos < n_lat)
    mask = jnp.concatenate([mask] * SWA_GROUP, axis=1)
    s_loc = jnp.where(mask, s_loc, NEG_INF)

    sink = jnp.concatenate(
        [jnp.full((1, tq), sink_ref[g * SWA_GROUP + j] * LOG2E, F32) for j in range(SWA_GROUP)], axis=1)
    m = jnp.maximum(jnp.maximum(jnp.max(s_ctx, axis=0, keepdims=True), jnp.max(s_loc, axis=0, keepdims=True)), sink)
    p_ctx = jnp.exp2(s_ctx - m)
    p_loc = jnp.exp2(s_loc - m)
    l = jnp.sum(p_ctx, axis=0, keepdims=True) + jnp.sum(p_loc, axis=0, keepdims=True) + jnp.exp2(sink - m)
    ot = (jnp.dot(vt_ref[0, 0, :, n_lat:n_lat + ctx_len], p_ctx.astype(BF16), preferred_element_type=F32)
          + jnp.dot(vt_ref[0, 0, :, pl.ds(ws, span)], p_loc.astype(BF16), preferred_element_type=F32))
    ot = ot * (1.0 / l)
    o_ref[0] = jnp.concatenate([ot[:, j * tq:(j + 1) * tq].T for j in range(SWA_GROUP)], axis=1).astype(o_ref.dtype)


def _swa(sink, qt, k, vt, *, n_lat, need_ctx):
    b, _, _, t = qt.shape
    tq = ROW_TM
    nq = (t if need_ctx else n_lat) // tq
    return pl.pallas_call(
        functools.partial(_swa_kernel, tq=tq, n_lat=n_lat, ctx_len=t - n_lat),
        grid=(b, SWA_KV_HEADS, nq),
        in_specs=[
            pl.BlockSpec(memory_space=pltpu.SMEM),
            pl.BlockSpec((1, SWA_GROUP, SWA_HD, tq), lambda bi, gi, i: (bi, gi, 0, i)),
            pl.BlockSpec((1, 1, t, SWA_HD), lambda bi, gi, i: (bi, gi, 0, 0)),
            pl.BlockSpec((1, 1, SWA_HD, t), lambda bi, gi, i: (bi, gi, 0, 0)),
        ],
        out_specs=pl.BlockSpec((1, tq, SWA_GROUP * SWA_HD), lambda bi, gi, i: (bi, i, gi)),
        out_shape=jax.ShapeDtypeStruct((b, t, SWA_HEADS * SWA_HD), BF16),
        name="swa",
        compiler_params=pltpu.CompilerParams(
            dimension_semantics=("parallel", "parallel", "arbitrary"), vmem_limit_bytes=VMEM_LIMIT),
    )(sink, qt, k, vt)


def _merge_kernel(*refs, final):
    if final:
        x_ref, mod_ref, ng_ref, ya_ref, yb_ref, yc_ref, wz_ref, wg_ref, wb_ref, wo_ref, fn_ref, o_ref = refs
    else:
        x_ref, mod_ref, ng_ref, ya_ref, yb_ref, yc_ref, wz_ref, wg_ref, wb_ref, wo_ref, o_ref = refs
    x = x_ref[0]
    d = x.shape[1]
    shift = mod_ref[0, 0, 0:1, :]
    scale = mod_ref[0, 0, 1:2, :]
    gate = mod_ref[0, 0, 2:3, :]
    h = (_rms(x) * ng_ref[...] * (1.0 + scale) + shift).astype(BF16)
    acc = jnp.zeros(x.shape, F32)
    for r, y_ref in enumerate((ya_ref, yb_ref, yc_ref)):
        w = y_ref.shape[2]
        z = jnp.dot(h, wz_ref[:, r * w:(r + 1) * w], preferred_element_type=F32)
        u = (y_ref[0].astype(F32) * (z * jax.nn.sigmoid(z))).astype(BF16)
        tr = jnp.dot(u, wb_ref[r], preferred_element_type=F32)
        gm = jnp.dot(h, wg_ref[:, r * d:(r + 1) * d], preferred_element_type=F32)
        acc = acc + jax.nn.sigmoid(gm) * tr
    out = jnp.dot(acc.astype(BF16), wo_ref[...], preferred_element_type=F32)
    xn = x + gate * out
    if final:
        xn = _rms(xn) * fn_ref[...]
    o_ref[0] = xn


def _merge(x_all, mod4, norm_g, ya, yb, yc, wts, *, n_lat, final_norm=None):
    b, t, d = x_all.shape
    tm = ROW_TM
    final = final_norm is not None
    rows = n_lat if final else t
    row_spec = pl.BlockSpec((1, tm, d), lambda bi, i: (bi, i, 0))
    in_specs = [
        row_spec, _mod_spec(d, n_lat // tm), _const_spec((1, d)),
        row_spec, row_spec, row_spec,
        _const_spec(wts["wz"].shape), _const_spec(wts["wg"].shape),
        _const_spec(wts["wb"].shape), _const_spec(wts["wo"].shape),
    ]
    args = [x_all, mod4, norm_g, ya, yb, yc, wts["wz"], wts["wg"], wts["wb"], wts["wo"]]
    if final:
        in_specs.append(_const_spec((1, d)))
        args.append(final_norm)
    return pl.pallas_call(
        functools.partial(_merge_kernel, final=final),
        grid=(b, rows // tm),
        in_specs=in_specs,
        out_specs=row_spec,
        out_shape=jax.ShapeDtypeStruct((b, rows, d), F32),
        input_output_aliases={} if final else {0: 0},
        name="merge",
        compiler_params=pltpu.CompilerParams(
            dimension_semantics=("parallel", "arbitrary"), vmem_limit_bytes=VMEM_LIMIT),
    )(*args)


def _layer_weights(l, w_in, mla_q_norm, mla_w_uq, mla_kv_norm, mla_w_ukv, w_branch, w_out):
    d = w_in.shape[1]
    splits = (MLA_Q_RANK, MLA_KV_RANK, MLA_ROPE,
              DIFF_HEADS * 2 * DIFF_HD, DIFF_HEADS * 2 * DIFF_HD, DIFF_HEADS * DIFF_V,
              SWA_HEADS * SWA_HD, SWA_KV_HEADS * SWA_HD, SWA_KV_HEADS * SWA_HD,
              N_BRANCH * w_branch.shape[2], N_BRANCH * d)
    offs = [0]
    for s in splits:
        offs.append(offs[-1] + s)
    w = w_in[l]
    col = lambda a, b_: w[:, offs[a]:offs[b_]]
    zeros = lambda n: jnp.zeros((d, n), w.dtype)
    w1 = jnp.concatenate([col(0, 2), zeros(MLA_NOPE), col(2, 3), zeros(LANE - MLA_NOPE - MLA_ROPE)], axis=1)
    qk = MLA_NOPE + MLA_ROPE
    wuq = jnp.pad(mla_w_uq[l].reshape(MLA_Q_RANK, MLA_HEADS, qk), ((0, 0), (0, 0), (0, LANE - qk)))
    wukv = mla_w_ukv[l].reshape(MLA_KV_RANK, MLA_HEADS, MLA_NOPE + MLA_V)
    wuk = jnp.pad(wukv[:, :, :MLA_NOPE], ((0, 0), (0, 0), (0, LANE - MLA_NOPE)))
    wuv = wukv[:, :, MLA_NOPE:]
    return {
        "w1": w1.astype(BF16),
        "qn": mla_q_norm[l].reshape(1, -1),
        "kvn": mla_kv_norm[l].reshape(1, -1),
        "wuq": wuq.reshape(MLA_Q_RANK, MLA_HEADS * LANE).astype(BF16),
        "wukv": jnp.concatenate([wuk.reshape(MLA_KV_RANK, -1), wuv.reshape(MLA_KV_RANK, -1)], axis=1).astype(BF16),
        "wd": col(3, 6).astype(BF16),
        "ws": col(6, 9).astype(BF16),
        "wz": col(9, 10).astype(BF16),
        "wg": col(10, 11).astype(BF16),
        "wb": w_branch[l].astype(BF16),
        "wo": w_out[l].astype(BF16),
    }


def _rope_tables(n_lat, ctx_len):
    rows = n_lat // GRID_W
    row = jnp.repeat(jnp.arange(rows), GRID_W).astype(F32)
    col = jnp.tile(jnp.arange(GRID_W), rows).astype(F32)

    def cs(rot_dim):
        n_freq = rot_dim // 4
        inv_freq = ROPE_BASE ** (-jnp.arange(n_freq, dtype=F32) / n_freq)
        ang = jnp.concatenate([row[:, None] * inv_freq, col[:, None] * inv_freq], axis=-1)
        cos = jnp.concatenate([jnp.cos(ang), jnp.ones((ctx_len, rot_dim // 2), F32)], axis=0)
        sin = jnp.concatenate([jnp.sin(ang), jnp.zeros((ctx_len, rot_dim // 2), F32)], axis=0)
        return cos, sin

    cos, sin = cs(SWA_HD)
    cch = jnp.tile(cos, (1, LANE // (SWA_HD // 2)))
    ssh = jnp.tile(jnp.concatenate([-sin, sin], axis=1), (1, LANE // SWA_HD))
    cos, sin = cs(MLA_ROPE)
    t = cos.shape[0]
    pad = LANE - MLA_NOPE - MLA_ROPE
    ccm = jnp.concatenate([jnp.ones((t, MLA_NOPE), F32), cos, cos, jnp.zeros((t, pad), F32)], axis=1)
    ssm = jnp.concatenate([jnp.zeros((t, MLA_NOPE), F32), -sin, sin, jnp.zeros((t, pad), F32)], axis=1)
    return {"cch": cch, "ssh": ssh, "ccm": ccm, "ssm": ssm}


def kernel(x, c, ctx, c_ctx, w_mod, b_mod, norm_g, w_in, mla_q_norm, mla_w_uq, mla_kv_norm, mla_w_ukv,
           diff_lq1, diff_lk1, diff_lq2, diff_lk2, diff_subln, swa_sink, w_branch, w_out, final_norm):
    b, n_lat, d = x.shape
    ctx_len = ctx.shape[1]
    depth = w_mod.shape[0]
    assert ctx_len == ROW_TM and n_lat % ROW_TM == 0 and n_lat % GRID_W == 0
    assert n_lat >= ROW_TM + 2 * WINDOW and (n_lat + ctx_len) % MXU_DIM == 0 and b < 16

    x_all = jnp.concatenate([x, ctx], axis=1)
    tabs = _rope_tables(n_lat, ctx_len)

    c_rows = jnp.concatenate([c, c_ctx[None, :], jnp.zeros((16 - b - 1, d), c.dtype)], axis=0)
    mod = _modulation(c_rows, w_mod, b_mod)

    out = None
    for l in range(depth):
        need_ctx = l < depth - 1
        mod_lat = mod[l, :b].reshape(b, 1, 3, d)
        mod_ctx = jnp.broadcast_to(mod[l, b].reshape(1, 1, 3, d), (b, 1, 3, d))
        mod4 = jnp.concatenate([mod_ctx, mod_lat], axis=1)
        wts = _layer_weights(l, w_in, mla_q_norm, mla_w_uq, mla_kv_norm, mla_w_ukv, w_branch, w_out)
        ng = norm_g[l].reshape(1, d)

        mqt, mk, mvt, dqt, dk, dvt, sqt, sk, svt = _project(x_all, mod4, ng, wts, tabs, n_lat)

        lam_init = 0.8 - 0.6 * math.exp(-0.3 * l)
        lam = (jnp.exp(jnp.sum(diff_lq1[l].astype(F32) * diff_lk1[l].astype(F32)))
               - jnp.exp(jnp.sum(diff_lq2[l].astype(F32) * diff_lk2[l].astype(F32))) + lam_init).reshape(1)

        ya = _flash(mqt, mk, mvt, n_sub=1, n_lat=n_lat, need_ctx=need_ctx)
        yb = _flash(dqt, dk, dvt, n_sub=2, n_lat=n_lat, need_ctx=need_ctx,
                    lam=lam, subln=diff_subln[l].reshape(LANE, 1), lam_scale=1.0 - lam_init)
        yc = _swa(swa_sink[l], sqt, sk, svt, n_lat=n_lat, need_ctx=need_ctx)

        if need_ctx:
            x_all = _merge(x_all, mod4, ng, ya, yb, yc, wts, n_lat=n_lat)
        else:
            out = _merge(x_all, mod4, ng, ya, yb, yc, wts, n_lat=n_lat, final_norm=final_norm.reshape(1, d))
    return out
```

```python
import functools
import math

import jax
import jax.numpy as jnp
from jax import lax
from jax.experimental import pallas as pl
from jax.experimental.pallas import tpu as pltpu

F32 = jnp.float32
BF16 = jnp.bfloat16

GRID_W = 64
N_BRANCH = 3
EPS = 1e-6
ROPE_BASE = 10000.0
NEG_INF = -1e30
LOG2E = math.log2(math.e)

MLA_HEADS = 8
MLA_NOPE = 64
MLA_ROPE = 32
MLA_V = 128
MLA_Q_RANK = 384
MLA_KV_RANK = 256

DIFF_HEADS = 8
DIFF_HD = 64
DIFF_V = 2 * DIFF_HD

SWA_HEADS = 16
SWA_KV_HEADS = 4
SWA_GROUP = SWA_HEADS // SWA_KV_HEADS
SWA_HD = 64
WINDOW = 128

LANE = 128
MXU_DIM = 256

VMEM_LIMIT = 56 * 1024 * 1024

ROW_TM = 256
FLASH_TQ_MLA = 1024
FLASH_TQ_DIFF = 512
FLASH_TK_MAX = 768
FLASH_UNROLL = 4


def _const_spec(shape):
    zeros = (0,) * len(shape)
    return pl.BlockSpec(shape, lambda *_: zeros, pipeline_mode=pl.Buffered(1))


def _rms(x, eps=EPS):
    return x * lax.rsqrt(jnp.mean(x * x, axis=-1, keepdims=True) + eps)


def _mod_kernel(c_ref, w_ref, b_ref, o_ref):
    cv = c_ref[...]
    s = cv * jax.nn.sigmoid(cv)
    o_ref[0] = jnp.dot(s, w_ref[0], preferred_element_type=F32) + b_ref[0]


def _modulation(c_rows, w_mod, b_mod):
    depth, d, d3 = w_mod.shape
    rows = c_rows.shape[0]
    tn = d
    return pl.pallas_call(
        _mod_kernel,
        grid=(depth, d3 // tn),
        in_specs=[
            pl.BlockSpec((rows, d), lambda l, n: (0, 0)),
            pl.BlockSpec((1, d, tn), lambda l, n: (l, 0, n)),
            pl.BlockSpec((1, 1, tn), lambda l, n: (l, 0, n)),
        ],
        out_specs=pl.BlockSpec((1, rows, tn), lambda l, n: (l, 0, n)),
        out_shape=jax.ShapeDtypeStruct((depth, rows, d3), F32),
        name="modulation",
        compiler_params=pltpu.CompilerParams(
            dimension_semantics=("arbitrary", "arbitrary"), vmem_limit_bytes=VMEM_LIMIT),
    )(c_rows, w_mod, b_mod.reshape(depth, 1, d3))


def _rope_lanes(x, cc, ss, lo_mask, half):
    partner = jnp.where(lo_mask, pltpu.roll(x, LANE - half, axis=1), pltpu.roll(x, half, axis=1))
    return x * cc + partner * ss


def _swap_halves(x, lo, half):
    return [x[lo + half:lo + 2 * half], x[lo:lo + half]]


def _rope_rows_hd(x, cc, ss):
    half = SWA_HD // 2
    partner = jnp.concatenate(_swap_halves(x, 0, half) + _swap_halves(x, 2 * half, half), axis=0)
    return x * cc + partner * ss


def _rope_rows_mla(x, cc, ss):
    half = MLA_ROPE // 2
    partner = jnp.concatenate(
        [x[:MLA_NOPE]] + _swap_halves(x, MLA_NOPE, half) + [x[MLA_NOPE + MLA_ROPE:]], axis=0)
    return x * cc + partner * ss


def _rms_rows(x, g):
    return x * lax.rsqrt(jnp.mean(x * x, axis=0, keepdims=True) + EPS) * g


def _proj_kernel(x_ref, mod_ref, ng_ref,
                 w1qt_ref, w1kvt_ref, w1n_ref, qng_ref, kvng_ref, kvn_ref, wuqt_ref, wuk_ref, wuvt_ref,
                 wdqt_ref, wdk_ref, wdvt_ref, wsqt_ref, wsk_ref, wsvt_ref,
                 cch_ref, ssh_ref, ccm_ref, ssm_ref, ccht_ref, ssht_ref, ccmt_ref, ssmt_ref,
                 mqt_ref, mk_ref, mvt_ref, dqt_ref, dk_ref, dvt_ref, sqt_ref, sk_ref, svt_ref):
    x = x_ref[0]
    shift = mod_ref[0, 0, 0:1, :]
    scale = mod_ref[0, 0, 1:2, :]
    h = _rms(x) * ng_ref[...] * (1.0 + scale) + shift
    hb = h.astype(BF16)
    ht = h.T.astype(BF16)
    tm = x.shape[0]
    reps = tm // LANE

    def dot(a, b):
        return jnp.dot(a, b, preferred_element_type=F32)

    lane = lax.broadcasted_iota(jnp.int32, (tm, LANE), 1)
    lo_hd = (lane & (SWA_HD // 2)) == 0
    lo_mla = (lane & (MLA_ROPE // 2)) == 0
    cch, ssh = cch_ref[...], ssh_ref[...]
    ccm, ssm = ccm_ref[...], ssm_ref[...]
    ccht, ssht = ccht_ref[...], ssht_ref[...]
    ccmt, ssmt = ccmt_ref[...], ssmt_ref[...]

    qnt = _rms_rows(dot(w1qt_ref[...], ht), jnp.concatenate([qng_ref[...]] * reps, axis=1)).astype(BF16)
    q_t = dot(wuqt_ref[...], qnt)
    kvnt = _rms_rows(dot(w1kvt_ref[...], ht), jnp.concatenate([kvng_ref[...]] * reps, axis=1)).astype(BF16)
    v_t = dot(wuvt_ref[...], kvnt)
    t1 = dot(hb, w1n_ref[...])
    kvn = (_rms(t1[:, :MLA_KV_RANK]) * kvn_ref[...]).astype(BF16)
    k_m = dot(kvn, wuk_ref[...])
    kr_roped = _rope_lanes(t1[:, MLA_KV_RANK:], ccm, ssm, lo_mla, MLA_ROPE // 2)
    for hh in range(MLA_HEADS):
        sl = slice(hh * LANE, (hh + 1) * LANE)
        mqt_ref[0, hh] = _rope_rows_mla(q_t[sl], ccmt, ssmt).astype(BF16)
        mk_ref[0, hh] = (k_m[:, sl] + kr_roped).astype(BF16)
        mvt_ref[0, hh] = v_t[sl].astype(BF16)

    q_t = dot(wdqt_ref[...], ht)
    k_n = dot(hb, wdk_ref[...])
    v_t = dot(wdvt_ref[...], ht)
    for hh in range(DIFF_HEADS):
        sl = slice(hh * LANE, (hh + 1) * LANE)
        dqt_ref[0, hh] = _rope_rows_hd(q_t[sl], ccht, ssht).astype(BF16)
        dk_ref[0, hh] = _rope_lanes(k_n[:, sl], cch, ssh, lo_hd, DIFF_HD // 2).astype(BF16)
        dvt_ref[0, hh] = v_t[sl].astype(BF16)

    q_t = dot(wsqt_ref[...], ht)
    k_n = dot(hb, wsk_ref[...])
    v_t = dot(wsvt_ref[...], ht)
    for j in range(SWA_HEADS * SWA_HD // LANE):
        qb = _rope_rows_hd(q_t[j * LANE:(j + 1) * LANE], ccht, ssht).astype(BF16)
        sqt_ref[0, 2 * j] = qb[:SWA_HD]
        sqt_ref[0, 2 * j + 1] = qb[SWA_HD:]
    for j in range(SWA_KV_HEADS * SWA_HD // LANE):
        kh = _rope_lanes(k_n[:, j * LANE:(j + 1) * LANE], cch, ssh, lo_hd, SWA_HD // 2).astype(BF16)
        sk_ref[0, 2 * j] = kh[:, :SWA_HD]
        sk_ref[0, 2 * j + 1] = kh[:, SWA_HD:]
    for g in range(SWA_KV_HEADS):
        svt_ref[0, g] = v_t[g * SWA_HD:(g + 1) * SWA_HD].astype(BF16)


def _mod_spec(d, lat_blk):
    return pl.BlockSpec((1, 1, 3, d), lambda bi, i: (bi, jnp.where(i < lat_blk, 1, 0), 0, 0))


_PROJ_WEIGHTS = ("w1qt", "w1kvt", "w1n", "qng", "kvng", "kvn", "wuqt", "wuk", "wuvt",
                 "wdqt", "wdk", "wdvt", "wsqt", "wsk", "wsvt")


def _project(x_all, mod4, norm_g, wts, tabs, n_lat):
    b, t, d = x_all.shape
    tm = ROW_TM

    def headmajor(nh, natural, width=LANE):
        if natural:
            return (jax.ShapeDtypeStruct((b, nh, t, width), BF16),
                    pl.BlockSpec((1, nh, tm, width), lambda bi, i: (bi, 0, i, 0)))
        return (jax.ShapeDtypeStruct((b, nh, width, t), BF16),
                pl.BlockSpec((1, nh, width, tm), lambda bi, i: (bi, 0, 0, i)))

    outs = [headmajor(MLA_HEADS, False), headmajor(MLA_HEADS, True), headmajor(MLA_HEADS, False),
            headmajor(DIFF_HEADS, False), headmajor(DIFF_HEADS, True), headmajor(DIFF_HEADS, False),
            headmajor(SWA_HEADS, False, SWA_HD), headmajor(SWA_KV_HEADS, True, SWA_HD),
            headmajor(SWA_KV_HEADS, False, SWA_HD)]
    tab_spec = pl.BlockSpec((tm, LANE), lambda bi, i: (i, 0))
    tabt_spec = pl.BlockSpec((LANE, tm), lambda bi, i: (0, i))
    in_specs = ([pl.BlockSpec((1, tm, d), lambda bi, i: (bi, i, 0)), _mod_spec(d, n_lat // tm), _const_spec((1, d))]
                + [_const_spec(wts[name].shape) for name in _PROJ_WEIGHTS]
                + [tab_spec] * 4 + [tabt_spec] * 4)
    return pl.pallas_call(
        _proj_kernel,
        grid=(b, t // tm),
        in_specs=in_specs,
        out_specs=[o[1] for o in outs],
        out_shape=[o[0] for o in outs],
        name="proj",
        compiler_params=pltpu.CompilerParams(
            dimension_semantics=("parallel", "arbitrary"), vmem_limit_bytes=VMEM_LIMIT),
    )(x_all, mod4, norm_g, *[wts[name] for name in _PROJ_WEIGHTS],
      tabs["cch"], tabs["ssh"], tabs["ccm"], tabs["ssm"], tabs["ccht"], tabs["ssht"], tabs["ccmt"], tabs["ssmt"])


def _stack_sub_heads(qt, n_sub):
    if n_sub == 1:
        return qt
    row = lax.broadcasted_iota(jnp.int32, qt.shape, 0)
    zero = jnp.zeros_like(qt)
    return jnp.concatenate([jnp.where(row < DIFF_HD, qt, zero), jnp.where(row >= DIFF_HD, qt, zero)], axis=1)


def _flash_finalize(o, o_ref, n_sub, tq, lam_ref, subln_ref, lam_scale):
    if n_sub == 2:
        lam = lam_ref[0]
        d = o[:, :tq] - lam * o[:, tq:]
        y = d * lax.rsqrt(jnp.mean(d * d, axis=0, keepdims=True) + EPS) * subln_ref[...] * lam_scale
    else:
        y = o
    o_ref[0] = y.T.astype(o_ref.dtype)


def _flash_kernel(*refs, n_sub, tq, tk, n_chunks, unroll, lam_scale):
    if n_sub == 2:
        lam_ref, qt_ref, k_ref, vt_ref, subln_ref, o_ref, acc_ref, s_ref = refs
    else:
        qt_ref, k_ref, vt_ref, o_ref, acc_ref, s_ref = refs
        lam_ref = subln_ref = None
    qt = _stack_sub_heads(qt_ref[0, 0], n_sub)
    r = qt.shape[1]

    def chunk_start(c):
        return pl.multiple_of(c * tk, LANE)

    def scores(c):
        kc = k_ref[0, 0, pl.ds(chunk_start(c), tk), :]
        return jnp.dot(kc, qt, preferred_element_type=F32)

    def process(s, m, l, c):
        m_new = jnp.maximum(m, jnp.max(s, axis=0, keepdims=True))
        p = jnp.exp2(s - m_new)
        alpha = jnp.exp2(m - m_new)
        l_new = alpha * l + jnp.sum(p, axis=0, keepdims=True)
        vc = vt_ref[0, 0, :, pl.ds(chunk_start(c), tk)]
        acc_ref[...] = alpha * acc_ref[...] + jnp.dot(vc, p.astype(BF16), preferred_element_type=F32)
        return m_new, l_new

    def run(c0, count, m, l, prefetch_after):
        for j in range(count):
            if j < count - 1 or prefetch_after:
                s_ref[(j + 1) % 2] = scores(c0 + j + 1)
            m, l = process(s_ref[j % 2], m, l, c0 + j)
        return m, l

    s_ref[0] = scores(0)
    acc_ref[...] = jnp.zeros_like(acc_ref)
    m = jnp.full((1, r), NEG_INF, F32)
    l = jnp.zeros((1, r), F32)
    n_loop = (n_chunks - 1) // unroll
    m, l = lax.fori_loop(0, n_loop, lambda g, ml: run(g * unroll, unroll, ml[0], ml[1], True), (m, l))
    m, l = run(n_loop * unroll, n_chunks - n_loop * unroll, m, l, False)

    _flash_finalize(acc_ref[...] * (1.0 / l), o_ref, n_sub, tq, lam_ref, subln_ref, lam_scale)


def _flash_ctx_kernel(*refs, n_sub, tq, lam_scale):
    if n_sub == 2:
        lam_ref, qt_ref, k_ref, vt_ref, subln_ref, _, o_ref = refs
    else:
        qt_ref, k_ref, vt_ref, _, o_ref = refs
        lam_ref = subln_ref = None
    qt = _stack_sub_heads(qt_ref[0, 0], n_sub)
    s = jnp.dot(k_ref[0, 0], qt, preferred_element_type=F32)
    p = jnp.exp2(s - jnp.max(s, axis=0, keepdims=True))
    l = jnp.sum(p, axis=0, keepdims=True)
    o = jnp.dot(vt_ref[0, 0], p.astype(BF16), preferred_element_type=F32) * (1.0 / l)
    _flash_finalize(o, o_ref, n_sub, tq, lam_ref, subln_ref, lam_scale)


def _chunk_size(t):
    return max(k for k in range(MXU_DIM, FLASH_TK_MAX + 1, MXU_DIM) if t % k == 0)


def _flash(qt, k, vt, *, n_sub, n_lat, need_ctx, lam=None, subln=None, lam_scale=1.0):
    b, nh, _, t = qt.shape
    ctx_len = t - n_lat
    tq = FLASH_TQ_DIFF if n_sub == 2 else FLASH_TQ_MLA
    tq = math.gcd(tq, n_lat)
    tk = _chunk_size(t)
    params = pltpu.CompilerParams(
        dimension_semantics=("parallel", "parallel", "arbitrary"), vmem_limit_bytes=VMEM_LIMIT)
    extra_in = [pl.BlockSpec(memory_space=pltpu.SMEM)] if n_sub == 2 else []
    extra_out = [_const_spec((LANE, 1))] if n_sub == 2 else []
    pre = [lam] if n_sub == 2 else []
    post = [subln] if n_sub == 2 else []
    name = "flash_diff" if n_sub == 2 else "flash_mla"

    y = pl.pallas_call(
        functools.partial(_flash_kernel, n_sub=n_sub, tq=tq, tk=tk, n_chunks=t // tk,
                          unroll=FLASH_UNROLL, lam_scale=lam_scale),
        grid=(b, nh, n_lat // tq),
        in_specs=extra_in + [
            pl.BlockSpec((1, 1, LANE, tq), lambda bi, hi, i: (bi, hi, 0, i)),
            pl.BlockSpec((1, 1, t, LANE), lambda bi, hi, i: (bi, hi, 0, 0)),
            pl.BlockSpec((1, 1, LANE, t), lambda bi, hi, i: (bi, hi, 0, 0)),
        ] + extra_out,
        out_specs=pl.BlockSpec((1, tq, LANE), lambda bi, hi, i: (bi, i, hi)),
        out_shape=jax.ShapeDtypeStruct((b, t, nh * LANE), BF16),
        scratch_shapes=[pltpu.VMEM((LANE, n_sub * tq), F32), pltpu.VMEM((2, tk, n_sub * tq), F32)],
        name=name,
        compiler_params=params,
    )(*pre, qt, k, vt, *post)
    if not need_ctx:
        return y

    cblk = n_lat // ctx_len
    return pl.pallas_call(
        functools.partial(_flash_ctx_kernel, n_sub=n_sub, tq=ctx_len, lam_scale=lam_scale),
        grid=(b, nh),
        in_specs=extra_in + [
            pl.BlockSpec((1, 1, LANE, ctx_len), lambda bi, hi: (bi, hi, 0, cblk)),
            pl.BlockSpec((1, 1, ctx_len, LANE), lambda bi, hi: (bi, hi, cblk, 0)),
            pl.BlockSpec((1, 1, LANE, ctx_len), lambda bi, hi: (bi, hi, 0, cblk)),
        ] + extra_out + [pl.BlockSpec(memory_space=pl.ANY)],
        out_specs=pl.BlockSpec((1, ctx_len, LANE), lambda bi, hi: (bi, cblk, hi)),
        out_shape=jax.ShapeDtypeStruct(y.shape, y.dtype),
        input_output_aliases={len(pre) + 3 + len(post): 0},
        name=name + "_ctx",
        compiler_params=pltpu.CompilerParams(
            dimension_semantics=("parallel", "arbitrary"), vmem_limit_bytes=VMEM_LIMIT),
    )(*pre, qt, k, vt, *post, y)


def _swa_kernel(sink_ref, qt_ref, k_ref, vt_ref, o_ref, *, tq, n_lat, ctx_len):
    g = pl.program_id(1)
    qs = pl.program_id(2) * tq
    sub = LANE
    span = sub + 2 * WINDOW
    n_sub = tq // sub
    width = SWA_GROUP * sub

    qt = jnp.concatenate([qt_ref[0, j][:, u * sub:(u + 1) * sub] for u in range(n_sub) for j in range(SWA_GROUP)],
                         axis=1)
    sink = jnp.concatenate(
        [jnp.full((1, sub), sink_ref[g * SWA_GROUP + j] * LOG2E, F32) for j in range(SWA_GROUP)], axis=1)
    s_ctx = jnp.dot(k_ref[0, 0, n_lat:n_lat + ctx_len, :], qt, preferred_element_type=F32)
    vt_ctx = vt_ref[0, 0, :, n_lat:n_lat + ctx_len]

    outs = []
    for u in range(n_sub):
        lanes = slice(u * width, (u + 1) * width)
        q0 = qs + u * sub
        ws = pl.multiple_of(jnp.clip(q0 - WINDOW, 0, n_lat - span), LANE)
        s_loc = jnp.dot(k_ref[0, 0, pl.ds(ws, span), :], qt[:, lanes], preferred_element_type=F32)
        kpos = ws + lax.broadcasted_iota(jnp.int32, (span, sub), 0)
        qpos = q0 + lax.broadcasted_iota(jnp.int32, (span, sub), 1)
        mask = (jnp.abs(qpos - kpos) <= WINDOW) & (qpos < n_lat)
        s_loc = jnp.where(jnp.concatenate([mask] * SWA_GROUP, axis=1), s_loc, NEG_INF)
        sc = s_ctx[:, lanes]
        m = jnp.maximum(jnp.maximum(jnp.max(sc, axis=0, keepdims=True), jnp.max(s_loc, axis=0, keepdims=True)), sink)
        p_ctx = jnp.exp2(sc - m)
        p_loc = jnp.exp2(s_loc - m)
        l = jnp.sum(p_ctx, axis=0, keepdims=True) + jnp.sum(p_loc, axis=0, keepdims=True) + jnp.exp2(sink - m)
        ot = (jnp.dot(vt_ctx, p_ctx.astype(BF16), preferred_element_type=F32)
              + jnp.dot(vt_ref[0, 0, :, pl.ds(ws, span)], p_loc.astype(BF16), preferred_element_type=F32))
        ot = ot * (1.0 / l)
        outs.append(jnp.concatenate([ot[:, j * sub:(j + 1) * sub].T for j in range(SWA_GROUP)], axis=1))
    o_ref[0] = jnp.concatenate(outs, axis=0).astype(o_ref.dtype)


def _swa(sink, qt, k, vt, *, n_lat, need_ctx):
    b, _, _, t = qt.shape
    tq = ROW_TM
    nq = (t if need_ctx else n_lat) // tq
    return pl.pallas_call(
        functools.partial(_swa_kernel, tq=tq, n_lat=n_lat, ctx_len=t - n_lat),
        grid=(b, SWA_KV_HEADS, nq),
        in_specs=[
            pl.BlockSpec(memory_space=pltpu.SMEM),
            pl.BlockSpec((1, SWA_GROUP, SWA_HD, tq), lambda bi, gi, i: (bi, gi, 0, i)),
            pl.BlockSpec((1, 1, t, SWA_HD), lambda bi, gi, i: (bi, gi, 0, 0)),
            pl.BlockSpec((1, 1, SWA_HD, t), lambda bi, gi, i: (bi, gi, 0, 0)),
        ],
        out_specs=pl.BlockSpec((1, tq, SWA_GROUP * SWA_HD), lambda bi, gi, i: (bi, i, gi)),
        out_shape=jax.ShapeDtypeStruct((b, t, SWA_HEADS * SWA_HD), BF16),
        name="swa",
        compiler_params=pltpu.CompilerParams(
            dimension_semantics=("parallel", "parallel", "arbitrary"), vmem_limit_bytes=VMEM_LIMIT),
    )(sink, qt, k, vt)


def _merge_kernel(*refs, final):
    if final:
        x_ref, mod_ref, ng_ref, ya_ref, yb_ref, yc_ref, wz_ref, wg_ref, wb_ref, wo_ref, fn_ref, o_ref = refs
    else:
        x_ref, mod_ref, ng_ref, ya_ref, yb_ref, yc_ref, wz_ref, wg_ref, wb_ref, wo_ref, o_ref = refs
    x = x_ref[0]
    d = x.shape[1]
    shift = mod_ref[0, 0, 0:1, :]
    scale = mod_ref[0, 0, 1:2, :]
    gate = mod_ref[0, 0, 2:3, :]
    h = (_rms(x) * ng_ref[...] * (1.0 + scale) + shift).astype(BF16)
    acc = jnp.zeros(x.shape, F32)
    for r, y_ref in enumerate((ya_ref, yb_ref, yc_ref)):
        w = y_ref.shape[2]
        z = jnp.dot(h, wz_ref[:, r * w:(r + 1) * w], preferred_element_type=F32)
        u = (y_ref[0].astype(F32) * (z * jax.nn.sigmoid(z))).astype(BF16)
        tr = jnp.dot(u, wb_ref[r], preferred_element_type=F32)
        gm = jnp.dot(h, wg_ref[:, r * d:(r + 1) * d], preferred_element_type=F32)
        acc = acc + jax.nn.sigmoid(gm) * tr
    out = jnp.dot(acc.astype(BF16), wo_ref[...], preferred_element_type=F32)
    xn = x + gate * out
    if final:
        xn = _rms(xn) * fn_ref[...]
    o_ref[0] = xn


def _merge(x_all, mod4, norm_g, ya, yb, yc, wts, *, n_lat, final_norm=None):
    b, t, d = x_all.shape
    tm = ROW_TM
    final = final_norm is not None
    rows = n_lat if final else t
    row_spec = pl.BlockSpec((1, tm, d), lambda bi, i: (bi, i, 0))
    in_specs = [
        row_spec, _mod_spec(d, n_lat // tm), _const_spec((1, d)),
        row_spec, row_spec, row_spec,
        _const_spec(wts["wz"].shape), _const_spec(wts["wg"].shape),
        _const_spec(wts["wb"].shape), _const_spec(wts["wo"].shape),
    ]
    args = [x_all, mod4, norm_g, ya, yb, yc, wts["wz"], wts["wg"], wts["wb"], wts["wo"]]
    if final:
        in_specs.append(_const_spec((1, d)))
        args.append(final_norm)
    return pl.pallas_call(
        functools.partial(_merge_kernel, final=final),
        grid=(b, rows // tm),
        in_specs=in_specs,
        out_specs=row_spec,
        out_shape=jax.ShapeDtypeStruct((b, rows, d), F32),
        input_output_aliases={} if final else {0: 0},
        name="merge",
        compiler_params=pltpu.CompilerParams(
            dimension_semantics=("parallel", "arbitrary"), vmem_limit_bytes=VMEM_LIMIT),
    )(*args)


def _layer_weights(l, w_in, mla_q_norm, mla_w_uq, mla_kv_norm, mla_w_ukv, w_branch, w_out):
    d = w_in.shape[1]
    splits = (MLA_Q_RANK, MLA_KV_RANK, MLA_ROPE,
              DIFF_HEADS * 2 * DIFF_HD, DIFF_HEADS * 2 * DIFF_HD, DIFF_HEADS * DIFF_V,
              SWA_HEADS * SWA_HD, SWA_KV_HEADS * SWA_HD, SWA_KV_HEADS * SWA_HD,
              N_BRANCH * w_branch.shape[2], N_BRANCH * d)
    offs = [0]
    for s in splits:
        offs.append(offs[-1] + s)
    w = w_in[l]
    col = lambda a, b_: w[:, offs[a]:offs[b_]]
    zeros = lambda n: jnp.zeros((d, n), w.dtype)
    mla_scale = (MLA_NOPE + MLA_ROPE) ** -0.5 * LOG2E
    hd_scale = SWA_HD ** -0.5 * LOG2E
    qk = MLA_NOPE + MLA_ROPE
    wuq = jnp.pad(mla_w_uq[l].reshape(MLA_Q_RANK, MLA_HEADS, qk), ((0, 0), (0, 0), (0, LANE - qk)))
    wukv = mla_w_ukv[l].reshape(MLA_KV_RANK, MLA_HEADS, MLA_NOPE + MLA_V)
    wuk = jnp.pad(wukv[:, :, :MLA_NOPE], ((0, 0), (0, 0), (0, LANE - MLA_NOPE)))
    wuv = wukv[:, :, MLA_NOPE:]
    w1n = jnp.concatenate([col(1, 2), zeros(MLA_NOPE), col(2, 3), zeros(LANE - qk)], axis=1)
    bcast = lambda g: jnp.broadcast_to(g[:, None], (g.shape[0], LANE))
    t_bf = lambda m: m.T.astype(BF16)
    return {
        "w1qt": t_bf(col(0, 1)), "w1kvt": t_bf(col(1, 2)), "w1n": w1n.astype(BF16),
        "qng": bcast(mla_q_norm[l]), "kvng": bcast(mla_kv_norm[l]), "kvn": mla_kv_norm[l].reshape(1, -1),
        "wuqt": t_bf(wuq.reshape(MLA_Q_RANK, MLA_HEADS * LANE) * mla_scale),
        "wuk": wuk.reshape(MLA_KV_RANK, -1).astype(BF16),
        "wuvt": t_bf(wuv.reshape(MLA_KV_RANK, -1)),
        "wdqt": t_bf(col(3, 4) * hd_scale), "wdk": col(4, 5).astype(BF16), "wdvt": t_bf(col(5, 6)),
        "wsqt": t_bf(col(6, 7) * hd_scale), "wsk": col(7, 8).astype(BF16), "wsvt": t_bf(col(8, 9)),
        "wz": col(9, 10).astype(BF16),
        "wg": col(10, 11).astype(BF16),
        "wb": w_branch[l].astype(BF16),
        "wo": w_out[l].astype(BF16),
    }


def _rope_tables(n_lat, ctx_len):
    rows = n_lat // GRID_W
    row = jnp.repeat(jnp.arange(rows), GRID_W).astype(F32)
    col = jnp.tile(jnp.arange(GRID_W), rows).astype(F32)

    def cs(rot_dim):
        n_freq = rot_dim // 4
        inv_freq = ROPE_BASE ** (-jnp.arange(n_freq, dtype=F32) / n_freq)
        ang = jnp.concatenate([row[:, None] * inv_freq, col[:, None] * inv_freq], axis=-1)
        cos = jnp.concatenate([jnp.cos(ang), jnp.ones((ctx_len, rot_dim // 2), F32)], axis=0)
        sin = jnp.concatenate([jnp.sin(ang), jnp.zeros((ctx_len, rot_dim // 2), F32)], axis=0)
        return cos, sin

    cos, sin = cs(SWA_HD)
    cch = jnp.tile(cos, (1, LANE // (SWA_HD // 2)))
    ssh = jnp.tile(jnp.concatenate([-sin, sin], axis=1), (1, LANE // SWA_HD))
    cos, sin = cs(MLA_ROPE)
    t = cos.shape[0]
    pad = LANE - MLA_NOPE - MLA_ROPE
    ccm = jnp.concatenate([jnp.ones((t, MLA_NOPE), F32), cos, cos, jnp.zeros((t, pad), F32)], axis=1)
    ssm = jnp.concatenate([jnp.zeros((t, MLA_NOPE), F32), -sin, sin, jnp.zeros((t, pad), F32)], axis=1)
    return {"cch": cch, "ssh": ssh, "ccm": ccm, "ssm": ssm,
            "ccht": cch.T, "ssht": ssh.T, "ccmt": ccm.T, "ssmt": ssm.T}


def kernel(x, c, ctx, c_ctx, w_mod, b_mod, norm_g, w_in, mla_q_norm, mla_w_uq, mla_kv_norm, mla_w_ukv,
           diff_lq1, diff_lk1, diff_lq2, diff_lk2, diff_subln, swa_sink, w_branch, w_out, final_norm):
    b, n_lat, d = x.shape
    ctx_len = ctx.shape[1]
    depth = w_mod.shape[0]
    assert ctx_len == ROW_TM and n_lat % ROW_TM == 0 and n_lat % GRID_W == 0
    assert n_lat >= ROW_TM + 2 * WINDOW and (n_lat + ctx_len) % MXU_DIM == 0 and b < 16

    x_all = jnp.concatenate([x, ctx], axis=1)
    tabs = _rope_tables(n_lat, ctx_len)

    c_rows = jnp.concatenate([c, c_ctx[None, :], jnp.zeros((16 - b - 1, d), c.dtype)], axis=0)
    mod = _modulation(c_rows, w_mod, b_mod)

    out = None
    for l in range(depth):
        need_ctx = l < depth - 1
        mod_lat = mod[l, :b].reshape(b, 1, 3, d)
        mod_ctx = jnp.broadcast_to(mod[l, b].reshape(1, 1, 3, d), (b, 1, 3, d))
        mod4 = jnp.concatenate([mod_ctx, mod_lat], axis=1)
        wts = _layer_weights(l, w_in, mla_q_norm, mla_w_uq, mla_kv_norm, mla_w_ukv, w_branch, w_out)
        ng = norm_g[l].reshape(1, d)

        mqt, mk, mvt, dqt, dk, dvt, sqt, sk, svt = _project(x_all, mod4, ng, wts, tabs, n_lat)

        lam_init = 0.8 - 0.6 * math.exp(-0.3 * l)
        lam = (jnp.exp(jnp.sum(diff_lq1[l].astype(F32) * diff_lk1[l].astype(F32)))
               - jnp.exp(jnp.sum(diff_lq2[l].astype(F32) * diff_lk2[l].astype(F32))) + lam_init).reshape(1)

        ya = _flash(mqt, mk, mvt, n_sub=1, n_lat=n_lat, need_ctx=need_ctx)
        yb = _flash(dqt, dk, dvt, n_sub=2, n_lat=n_lat, need_ctx=need_ctx,
                    lam=lam, subln=diff_subln[l].reshape(LANE, 1), lam_scale=1.0 - lam_init)
        yc = _swa(swa_sink[l], sqt, sk, svt, n_lat=n_lat, need_ctx=need_ctx)

        if need_ctx:
            x_all = _merge(x_all, mod4, ng, ya, yb, yc, wts, n_lat=n_lat)
        else:
            out = _merge(x_all, mod4, ng, ya, yb, yc, wts, n_lat=n_lat, final_norm=final_norm.reshape(1, d))
    return out
```

```python
import functools
import math

import jax
import jax.numpy as jnp
from jax import lax
from jax.experimental import pallas as pl
from jax.experimental.pallas import tpu as pltpu

F32 = jnp.float32
BF16 = jnp.bfloat16

GRID_W = 64
N_BRANCH = 3
EPS = 1e-6
ROPE_BASE = 10000.0
NEG_INF = -1e30
LOG2E = math.log2(math.e)

MLA_HEADS = 8
MLA_NOPE = 64
MLA_ROPE = 32
MLA_V = 128
MLA_Q_RANK = 384
MLA_KV_RANK = 256

DIFF_HEADS = 8
DIFF_HD = 64
DIFF_V = 2 * DIFF_HD

SWA_HEADS = 16
SWA_KV_HEADS = 4
SWA_GROUP = SWA_HEADS // SWA_KV_HEADS
SWA_HD = 64
WINDOW = 128

LANE = 128
MXU_DIM = 256

VMEM_LIMIT = 56 * 1024 * 1024

ROW_TM = 256
FLASH_TQ_MLA = 1024
FLASH_TQ_DIFF = 512
FLASH_TK_MAX = 768
FLASH_UNROLL = 4


def _const_spec(shape):
    zeros = (0,) * len(shape)
    return pl.BlockSpec(shape, lambda *_: zeros, pipeline_mode=pl.Buffered(1))


def _rms(x, eps=EPS):
    return x * lax.rsqrt(jnp.mean(x * x, axis=-1, keepdims=True) + eps)


def _mod_kernel(c_ref, w_ref, b_ref, o_ref):
    cv = c_ref[...]
    s = cv * jax.nn.sigmoid(cv)
    o_ref[0] = jnp.dot(s, w_ref[0], preferred_element_type=F32) + b_ref[0]


def _modulation(c_rows, w_mod, b_mod):
    depth, d, d3 = w_mod.shape
    rows = c_rows.shape[0]
    tn = d
    return pl.pallas_call(
        _mod_kernel,
        grid=(depth, d3 // tn),
        in_specs=[
            pl.BlockSpec((rows, d), lambda l, n: (0, 0)),
            pl.BlockSpec((1, d, tn), lambda l, n: (l, 0, n)),
            pl.BlockSpec((1, 1, tn), lambda l, n: (l, 0, n)),
        ],
        out_specs=pl.BlockSpec((1, rows, tn), lambda l, n: (l, 0, n)),
        out_shape=jax.ShapeDtypeStruct((depth, rows, d3), F32),
        name="modulation",
        compiler_params=pltpu.CompilerParams(
            dimension_semantics=("arbitrary", "arbitrary"), vmem_limit_bytes=VMEM_LIMIT),
    )(c_rows, w_mod, b_mod.reshape(depth, 1, d3))


def _rope_lanes(x, cc, ss, lo_mask, half):
    partner = jnp.where(lo_mask, pltpu.roll(x, LANE - half, axis=1), pltpu.roll(x, half, axis=1))
    return x * cc + partner * ss


def _swap_halves(x, lo, half):
    return [x[lo + half:lo + 2 * half], x[lo:lo + half]]


def _rope_rows_hd(x, cc, ss):
    half = SWA_HD // 2
    partner = jnp.concatenate(_swap_halves(x, 0, half) + _swap_halves(x, 2 * half, half), axis=0)
    return x * cc + partner * ss


def _rope_rows_mla(x, cc, ss):
    half = MLA_ROPE // 2
    partner = jnp.concatenate(
        [x[:MLA_NOPE]] + _swap_halves(x, MLA_NOPE, half) + [x[MLA_NOPE + MLA_ROPE:]], axis=0)
    return x * cc + partner * ss


def _rms_rows(x, g):
    return x * lax.rsqrt(jnp.mean(x * x, axis=0, keepdims=True) + EPS) * g


def _proj_kernel(x_ref, mod_ref, ng_ref,
                 w1qt_ref, w1kvt_ref, w1n_ref, qng_ref, kvng_ref, kvn_ref, wuqt_ref, wuk_ref, wuvt_ref,
                 wdqt_ref, wdk_ref, wdvt_ref, wsqt_ref, wsk_ref, wsvt_ref,
                 cch_ref, ssh_ref, ccm_ref, ssm_ref, ccht_ref, ssht_ref, ccmt_ref, ssmt_ref,
                 mqt_ref, mk_ref, mvt_ref, dqt_ref, dk_ref, dvt_ref, sqt_ref, sk_ref, svt_ref):
    x = x_ref[0]
    shift = mod_ref[0, 0, 0:1, :]
    scale = mod_ref[0, 0, 1:2, :]
    h = _rms(x) * ng_ref[...] * (1.0 + scale) + shift
    hb = h.astype(BF16)
    ht = h.T.astype(BF16)
    tm = x.shape[0]
    reps = tm // LANE

    def dot(a, b):
        return jnp.dot(a, b, preferred_element_type=F32)

    lane = lax.broadcasted_iota(jnp.int32, (tm, LANE), 1)
    lo_hd = (lane & (SWA_HD // 2)) == 0
    lo_mla = (lane & (MLA_ROPE // 2)) == 0
    cch, ssh = cch_ref[...], ssh_ref[...]
    ccm, ssm = ccm_ref[...], ssm_ref[...]
    ccht, ssht = ccht_ref[...], ssht_ref[...]
    ccmt, ssmt = ccmt_ref[...], ssmt_ref[...]

    qnt = _rms_rows(dot(w1qt_ref[...], ht), jnp.concatenate([qng_ref[...]] * reps, axis=1)).astype(BF16)
    q_t = dot(wuqt_ref[...], qnt)
    kvnt = _rms_rows(dot(w1kvt_ref[...], ht), jnp.concatenate([kvng_ref[...]] * reps, axis=1)).astype(BF16)
    v_t = dot(wuvt_ref[...], kvnt)
    t1 = dot(hb, w1n_ref[...])
    kvn = (_rms(t1[:, :MLA_KV_RANK]) * kvn_ref[...]).astype(BF16)
    k_m = dot(kvn, wuk_ref[...])
    kr_roped = _rope_lanes(t1[:, MLA_KV_RANK:], ccm, ssm, lo_mla, MLA_ROPE // 2)
    for hh in range(MLA_HEADS):
        sl = slice(hh * LANE, (hh + 1) * LANE)
        mqt_ref[0, hh] = _rope_rows_mla(q_t[sl], ccmt, ssmt).astype(BF16)
        mk_ref[0, hh] = (k_m[:, sl] + kr_roped).astype(BF16)
        mvt_ref[0, hh] = v_t[sl].astype(BF16)

    q_t = dot(wdqt_ref[...], ht)
    k_n = dot(hb, wdk_ref[...])
    v_t = dot(wdvt_ref[...], ht)
    for hh in range(DIFF_HEADS):
        sl = slice(hh * LANE, (hh + 1) * LANE)
        dqt_ref[0, hh] = _rope_rows_hd(q_t[sl], ccht, ssht).astype(BF16)
        dk_ref[0, hh] = _rope_lanes(k_n[:, sl], cch, ssh, lo_hd, DIFF_HD // 2).astype(BF16)
        dvt_ref[0, hh] = v_t[sl].astype(BF16)

    q_t = dot(wsqt_ref[...], ht)
    k_n = dot(hb, wsk_ref[...])
    v_t = dot(wsvt_ref[...], ht)
    for j in range(SWA_HEADS * SWA_HD // LANE):
        qb = _rope_rows_hd(q_t[j * LANE:(j + 1) * LANE], ccht, ssht).astype(BF16)
        sqt_ref[0, 2 * j] = qb[:SWA_HD]
        sqt_ref[0, 2 * j + 1] = qb[SWA_HD:]
    for j in range(SWA_KV_HEADS * SWA_HD // LANE):
        kh = _rope_lanes(k_n[:, j * LANE:(j + 1) * LANE], cch, ssh, lo_hd, SWA_HD // 2).astype(BF16)
        sk_ref[0, 2 * j] = kh[:, :SWA_HD]
        sk_ref[0, 2 * j + 1] = kh[:, SWA_HD:]
    for g in range(SWA_KV_HEADS):
        svt_ref[0, g] = v_t[g * SWA_HD:(g + 1) * SWA_HD].astype(BF16)


def _mod_spec(d, lat_blk):
    return pl.BlockSpec((1, 1, 3, d), lambda bi, i: (bi, jnp.where(i < lat_blk, 1, 0), 0, 0))


_PROJ_WEIGHTS = ("w1qt", "w1kvt", "w1n", "qng", "kvng", "kvn", "wuqt", "wuk", "wuvt",
                 "wdqt", "wdk", "wdvt", "wsqt", "wsk", "wsvt")


def _project(x_all, mod4, norm_g, wts, tabs, n_lat):
    b, t, d = x_all.shape
    tm = ROW_TM

    def headmajor(nh, natural, width=LANE):
        if natural:
            return (jax.ShapeDtypeStruct((b, nh, t, width), BF16),
                    pl.BlockSpec((1, nh, tm, width), lambda bi, i: (bi, 0, i, 0)))
        return (jax.ShapeDtypeStruct((b, nh, width, t), BF16),
                pl.BlockSpec((1, nh, width, tm), lambda bi, i: (bi, 0, 0, i)))

    outs = [headmajor(MLA_HEADS, False), headmajor(MLA_HEADS, True), headmajor(MLA_HEADS, False),
            headmajor(DIFF_HEADS, False), headmajor(DIFF_HEADS, True), headmajor(DIFF_HEADS, False),
            headmajor(SWA_HEADS, False, SWA_HD), headmajor(SWA_KV_HEADS, True, SWA_HD),
            headmajor(SWA_KV_HEADS, False, SWA_HD)]
    tab_spec = pl.BlockSpec((tm, LANE), lambda bi, i: (i, 0))
    tabt_spec = pl.BlockSpec((LANE, tm), lambda bi, i: (0, i))
    in_specs = ([pl.BlockSpec((1, tm, d), lambda bi, i: (bi, i, 0)), _mod_spec(d, n_lat // tm), _const_spec((1, d))]
                + [_const_spec(wts[name].shape) for name in _PROJ_WEIGHTS]
                + [tab_spec] * 4 + [tabt_spec] * 4)
    return pl.pallas_call(
        _proj_kernel,
        grid=(b, t // tm),
        in_specs=in_specs,
        out_specs=[o[1] for o in outs],
        out_shape=[o[0] for o in outs],
        name="proj",
        compiler_params=pltpu.CompilerParams(
            dimension_semantics=("parallel", "arbitrary"), vmem_limit_bytes=VMEM_LIMIT),
    )(x_all, mod4, norm_g, *[wts[name] for name in _PROJ_WEIGHTS],
      tabs["cch"], tabs["ssh"], tabs["ccm"], tabs["ssm"], tabs["ccht"], tabs["ssht"], tabs["ccmt"], tabs["ssmt"])


def _stack_sub_heads(qt, n_sub):
    if n_sub == 1:
        return qt
    row = lax.broadcasted_iota(jnp.int32, qt.shape, 0)
    zero = jnp.zeros_like(qt)
    return jnp.concatenate([jnp.where(row < DIFF_HD, qt, zero), jnp.where(row >= DIFF_HD, qt, zero)], axis=1)


def _flash_finalize(o, o_ref, n_sub, tq, lam_ref, subln_ref, lam_scale):
    if n_sub == 2:
        lam = lam_ref[0]
        d = o[:, :tq] - lam * o[:, tq:]
        y = d * lax.rsqrt(jnp.mean(d * d, axis=0, keepdims=True) + EPS) * subln_ref[...] * lam_scale
    else:
        y = o
    o_ref[0] = y.T.astype(o_ref.dtype)


def _flash_kernel(*refs, n_sub, tq, tk, n_chunks, unroll, lam_scale):
    if n_sub == 2:
        lam_ref, qt_ref, k_ref, vt_ref, subln_ref, o_ref, acc_ref, s_ref = refs
    else:
        qt_ref, k_ref, vt_ref, o_ref, acc_ref, s_ref = refs
        lam_ref = subln_ref = None
    qt = _stack_sub_heads(qt_ref[0, 0], n_sub)
    r = qt.shape[1]

    def chunk_start(c):
        return pl.multiple_of(c * tk, LANE)

    def scores(c):
        kc = k_ref[0, 0, pl.ds(chunk_start(c), tk), :]
        return jnp.dot(kc, qt, preferred_element_type=F32)

    def process(s, m, l, c):
        m_new = jnp.maximum(m, jnp.max(s, axis=0, keepdims=True))
        p = jnp.exp2(s - m_new)
        alpha = jnp.exp2(m - m_new)
        l_new = alpha * l + jnp.sum(p, axis=0, keepdims=True)
        vc = vt_ref[0, 0, :, pl.ds(chunk_start(c), tk)]
        acc_ref[...] = alpha * acc_ref[...] + jnp.dot(vc, p.astype(BF16), preferred_element_type=F32)
        return m_new, l_new

    def run(c0, count, m, l, prefetch_after):
        for j in range(count):
            if j < count - 1 or prefetch_after:
                s_ref[(j + 1) % 2] = scores(c0 + j + 1)
            m, l = process(s_ref[j % 2], m, l, c0 + j)
        return m, l

    s_ref[0] = scores(0)
    acc_ref[...] = jnp.zeros_like(acc_ref)
    m = jnp.full((1, r), NEG_INF, F32)
    l = jnp.zeros((1, r), F32)
    n_loop = (n_chunks - 1) // unroll
    m, l = lax.fori_loop(0, n_loop, lambda g, ml: run(g * unroll, unroll, ml[0], ml[1], True), (m, l))
    m, l = run(n_loop * unroll, n_chunks - n_loop * unroll, m, l, False)

    _flash_finalize(acc_ref[...] * (1.0 / l), o_ref, n_sub, tq, lam_ref, subln_ref, lam_scale)


def _flash_ctx_kernel(*refs, n_sub, tq, lam_scale):
    if n_sub == 2:
        lam_ref, qt_ref, k_ref, vt_ref, subln_ref, _, o_ref = refs
    else:
        qt_ref, k_ref, vt_ref, _, o_ref = refs
        lam_ref = subln_ref = None
    qt = _stack_sub_heads(qt_ref[0, 0], n_sub)
    s = jnp.dot(k_ref[0, 0], qt, preferred_element_type=F32)
    p = jnp.exp2(s - jnp.max(s, axis=0, keepdims=True))
    l = jnp.sum(p, axis=0, keepdims=True)
    o = jnp.dot(vt_ref[0, 0], p.astype(BF16), preferred_element_type=F32) * (1.0 / l)
    _flash_finalize(o, o_ref, n_sub, tq, lam_ref, subln_ref, lam_scale)


def _chunk_size(t):
    return max(k for k in range(MXU_DIM, FLASH_TK_MAX + 1, MXU_DIM) if t % k == 0)


def _flash(qt, k, vt, *, n_sub, n_lat, need_ctx, lam=None, subln=None, lam_scale=1.0):
    b, nh, _, t = qt.shape
    ctx_len = t - n_lat
    tq = FLASH_TQ_DIFF if n_sub == 2 else FLASH_TQ_MLA
    tq = math.gcd(tq, n_lat)
    tk = _chunk_size(t)
    params = pltpu.CompilerParams(
        dimension_semantics=("parallel", "parallel", "arbitrary"), vmem_limit_bytes=VMEM_LIMIT)
    extra_in = [pl.BlockSpec(memory_space=pltpu.SMEM)] if n_sub == 2 else []
    extra_out = [_const_spec((LANE, 1))] if n_sub == 2 else []
    pre = [lam] if n_sub == 2 else []
    post = [subln] if n_sub == 2 else []
    name = "flash_diff" if n_sub == 2 else "flash_mla"

    y = pl.pallas_call(
        functools.partial(_flash_kernel, n_sub=n_sub, tq=tq, tk=tk, n_chunks=t // tk,
                          unroll=FLASH_UNROLL, lam_scale=lam_scale),
        grid=(b, nh, n_lat // tq),
        in_specs=extra_in + [
            pl.BlockSpec((1, 1, LANE, tq), lambda bi, hi, i: (bi, hi, 0, i)),
            pl.BlockSpec((1, 1, t, LANE), lambda bi, hi, i: (bi, hi, 0, 0)),
            pl.BlockSpec((1, 1, LANE, t), lambda bi, hi, i: (bi, hi, 0, 0)),
        ] + extra_out,
        out_specs=pl.BlockSpec((1, tq, LANE), lambda bi, hi, i: (bi, i, hi)),
        out_shape=jax.ShapeDtypeStruct((b, t, nh * LANE), BF16),
        scratch_shapes=[pltpu.VMEM((LANE, n_sub * tq), F32), pltpu.VMEM((2, tk, n_sub * tq), F32)],
        name=name,
        compiler_params=params,
    )(*pre, qt, k, vt, *post)
    if not need_ctx:
        return y

    cblk = n_lat // ctx_len
    return pl.pallas_call(
        functools.partial(_flash_ctx_kernel, n_sub=n_sub, tq=ctx_len, lam_scale=lam_scale),
        grid=(b, nh),
        in_specs=extra_in + [
            pl.BlockSpec((1, 1, LANE, ctx_len), lambda bi, hi: (bi, hi, 0, cblk)),
            pl.BlockSpec((1, 1, ctx_len, LANE), lambda bi, hi: (bi, hi, cblk, 0)),
            pl.BlockSpec((1, 1, LANE, ctx_len), lambda bi, hi: (bi, hi, 0, cblk)),
        ] + extra_out + [pl.BlockSpec(memory_space=pl.ANY)],
        out_specs=pl.BlockSpec((1, ctx_len, LANE), lambda bi, hi: (bi, cblk, hi)),
        out_shape=jax.ShapeDtypeStruct(y.shape, y.dtype),
        input_output_aliases={len(pre) + 3 + len(post): 0},
        name=name + "_ctx",
        compiler_params=pltpu.CompilerParams(
            dimension_semantics=("parallel", "arbitrary"), vmem_limit_bytes=VMEM_LIMIT),
    )(*pre, qt, k, vt, *post, y)


def _swa_kernel(sink_ref, qt_ref, k_ref, vt_ref, o_ref, s_ref, *, tq, n_lat, ctx_len, n_blocks, unroll):
    g = pl.program_id(1)
    sub = LANE
    span = sub + 2 * WINDOW
    n_sub = tq // sub
    width = SWA_GROUP * sub
    sink = jnp.concatenate(
        [jnp.full((1, sub), sink_ref[g * SWA_GROUP + j] * LOG2E, F32) for j in range(SWA_GROUP)], axis=1)

    def window(i, u):
        q0 = i * tq + u * sub
        return q0, pl.multiple_of(jnp.clip(q0 - WINDOW, 0, n_lat - span), LANE)

    def scores(i, slot):
        qs = pl.multiple_of(i * tq, tq)
        heads = [qt_ref[0, j, :, pl.ds(qs, tq)] for j in range(SWA_GROUP)]
        qt = jnp.concatenate([hq[:, u * sub:(u + 1) * sub] for u in range(n_sub) for hq in heads], axis=1)
        s_ref[slot, 0:ctx_len, :] = jnp.dot(k_ref[0, 0, n_lat:n_lat + ctx_len, :], qt, preferred_element_type=F32)
        for u in range(n_sub):
            _, ws = window(i, u)
            s_ref[slot, ctx_len:ctx_len + span, u * width:(u + 1) * width] = jnp.dot(
                k_ref[0, 0, pl.ds(ws, span), :], qt[:, u * width:(u + 1) * width], preferred_element_type=F32)

    def process(i, slot):
        outs = []
        for u in range(n_sub):
            lanes = slice(u * width, (u + 1) * width)
            q0, ws = window(i, u)
            kpos = ws + lax.broadcasted_iota(jnp.int32, (span, sub), 0)
            qpos = q0 + lax.broadcasted_iota(jnp.int32, (span, sub), 1)
            mask = (jnp.abs(qpos - kpos) <= WINDOW) & (qpos < n_lat)
            s_loc = jnp.where(jnp.concatenate([mask] * SWA_GROUP, axis=1),
                              s_ref[slot, ctx_len:ctx_len + span, lanes], NEG_INF)
            sc = s_ref[slot, 0:ctx_len, lanes]
            m = jnp.maximum(jnp.maximum(jnp.max(sc, axis=0, keepdims=True), jnp.max(s_loc, axis=0, keepdims=True)),
                            sink)
            p_ctx = jnp.exp2(sc - m)
            p_loc = jnp.exp2(s_loc - m)
            l = jnp.sum(p_ctx, axis=0, keepdims=True) + jnp.sum(p_loc, axis=0, keepdims=True) + jnp.exp2(sink - m)
            ot = (jnp.dot(vt_ref[0, 0, :, n_lat:n_lat + ctx_len], p_ctx.astype(BF16), preferred_element_type=F32)
                  + jnp.dot(vt_ref[0, 0, :, pl.ds(ws, span)], p_loc.astype(BF16), preferred_element_type=F32))
            ot = ot * (1.0 / l)
            outs.append(jnp.concatenate([ot[:, j * sub:(j + 1) * sub].T for j in range(SWA_GROUP)], axis=1))
        o_ref[0, pl.ds(pl.multiple_of(i * tq, tq), tq), :] = jnp.concatenate(outs, axis=0).astype(o_ref.dtype)

    def run(i0, count, prefetch_after):
        for j in range(count):
            if j < count - 1 or prefetch_after:
                scores(i0 + j + 1, (j + 1) % 2)
            process(i0 + j, j % 2)

    scores(0, 0)
    n_loop = (n_blocks - 1) // unroll

    def body(t, carry):
        run(t * unroll, unroll, True)
        return carry

    lax.fori_loop(0, n_loop, body, 0)
    run(n_loop * unroll, n_blocks - n_loop * unroll, False)
    if n_blocks * tq < o_ref.shape[1]:
        o_ref[0, n_blocks * tq:, :] = jnp.zeros((o_ref.shape[1] - n_blocks * tq, o_ref.shape[2]), o_ref.dtype)


def _swa(sink, qt, k, vt, *, n_lat, need_ctx):
    b, _, _, t = qt.shape
    tq = ROW_TM
    ctx_len = t - n_lat
    span = LANE + 2 * WINDOW
    return pl.pallas_call(
        functools.partial(_swa_kernel, tq=tq, n_lat=n_lat, ctx_len=ctx_len,
                          n_blocks=(t if need_ctx else n_lat) // tq, unroll=2),
        grid=(b, SWA_KV_HEADS),
        in_specs=[
            pl.BlockSpec(memory_space=pltpu.SMEM),
            pl.BlockSpec((1, SWA_GROUP, SWA_HD, t), lambda bi, gi: (bi, gi, 0, 0)),
            pl.BlockSpec((1, 1, t, SWA_HD), lambda bi, gi: (bi, gi, 0, 0)),
            pl.BlockSpec((1, 1, SWA_HD, t), lambda bi, gi: (bi, gi, 0, 0)),
        ],
        out_specs=pl.BlockSpec((1, t, SWA_GROUP * SWA_HD), lambda bi, gi: (bi, 0, gi)),
        out_shape=jax.ShapeDtypeStruct((b, t, SWA_HEADS * SWA_HD), BF16),
        scratch_shapes=[pltpu.VMEM((2, ctx_len + span, SWA_GROUP * tq), F32)],
        name="swa",
        compiler_params=pltpu.CompilerParams(
            dimension_semantics=("parallel", "arbitrary"), vmem_limit_bytes=VMEM_LIMIT),
    )(sink, qt, k, vt)


def _merge_kernel(*refs, final):
    if final:
        x_ref, mod_ref, ng_ref, ya_ref, yb_ref, yc_ref, wz_ref, wg_ref, wb_ref, wo_ref, fn_ref, o_ref = refs
    else:
        x_ref, mod_ref, ng_ref, ya_ref, yb_ref, yc_ref, wz_ref, wg_ref, wb_ref, wo_ref, o_ref = refs
    x = x_ref[0]
    d = x.shape[1]
    shift = mod_ref[0, 0, 0:1, :]
    scale = mod_ref[0, 0, 1:2, :]
    gate = mod_ref[0, 0, 2:3, :]
    h = (_rms(x) * ng_ref[...] * (1.0 + scale) + shift).astype(BF16)
    acc = jnp.zeros(x.shape, F32)
    for r, y_ref in enumerate((ya_ref, yb_ref, yc_ref)):
        w = y_ref.shape[2]
        z = jnp.dot(h, wz_ref[:, r * w:(r + 1) * w], preferred_element_type=F32)
        u = (y_ref[0].astype(F32) * (z * jax.nn.sigmoid(z))).astype(BF16)
        tr = jnp.dot(u, wb_ref[r], preferred_element_type=F32)
        gm = jnp.dot(h, wg_ref[:, r * d:(r + 1) * d], preferred_element_type=F32)
        acc = acc + jax.nn.sigmoid(gm) * tr
    out = jnp.dot(acc.astype(BF16), wo_ref[...], preferred_element_type=F32)
    xn = x + gate * out
    if final:
        xn = _rms(xn) * fn_ref[...]
    o_ref[0] = xn


def _merge(x_all, mod4, norm_g, ya, yb, yc, wts, *, n_lat, final_norm=None):
    b, t, d = x_all.shape
    tm = ROW_TM
    final = final_norm is not None
    rows = n_lat if final else t
    row_spec = pl.BlockSpec((1, tm, d), lambda bi, i: (bi, i, 0))
    in_specs = [
        row_spec, _mod_spec(d, n_lat // tm), _const_spec((1, d)),
        row_spec, row_spec, row_spec,
        _const_spec(wts["wz"].shape), _const_spec(wts["wg"].shape),
        _const_spec(wts["wb"].shape), _const_spec(wts["wo"].shape),
    ]
    args = [x_all, mod4, norm_g, ya, yb, yc, wts["wz"], wts["wg"], wts["wb"], wts["wo"]]
    if final:
        in_specs.append(_const_spec((1, d)))
        args.append(final_norm)
    return pl.pallas_call(
        functools.partial(_merge_kernel, final=final),
        grid=(b, rows // tm),
        in_specs=in_specs,
        out_specs=row_spec,
        out_shape=jax.ShapeDtypeStruct((b, rows, d), F32),
        input_output_aliases={} if final else {0: 0},
        name="merge",
        compiler_params=pltpu.CompilerParams(
            dimension_semantics=("parallel", "arbitrary"), vmem_limit_bytes=VMEM_LIMIT),
    )(*args)


def _layer_weights(l, w_in, mla_q_norm, mla_w_uq, mla_kv_norm, mla_w_ukv, w_branch, w_out):
    d = w_in.shape[1]
    splits = (MLA_Q_RANK, MLA_KV_RANK, MLA_ROPE,
              DIFF_HEADS * 2 * DIFF_HD, DIFF_HEADS * 2 * DIFF_HD, DIFF_HEADS * DIFF_V,
              SWA_HEADS * SWA_HD, SWA_KV_HEADS * SWA_HD, SWA_KV_HEADS * SWA_HD,
              N_BRANCH * w_branch.shape[2], N_BRANCH * d)
    offs = [0]
    for s in splits:
        offs.append(offs[-1] + s)
    w = w_in[l]
    col = lambda a, b_: w[:, offs[a]:offs[b_]]
    zeros = lambda n: jnp.zeros((d, n), w.dtype)
    mla_scale = (MLA_NOPE + MLA_ROPE) ** -0.5 * LOG2E
    hd_scale = SWA_HD ** -0.5 * LOG2E
    qk = MLA_NOPE + MLA_ROPE
    wuq = jnp.pad(mla_w_uq[l].reshape(MLA_Q_RANK, MLA_HEADS, qk), ((0, 0), (0, 0), (0, LANE - qk)))
    wukv = mla_w_ukv[l].reshape(MLA_KV_RANK, MLA_HEADS, MLA_NOPE + MLA_V)
    wuk = jnp.pad(wukv[:, :, :MLA_NOPE], ((0, 0), (0, 0), (0, LANE - MLA_NOPE)))
    wuv = wukv[:, :, MLA_NOPE:]
    w1n = jnp.concatenate([col(1, 2), zeros(MLA_NOPE), col(2, 3), zeros(LANE - qk)], axis=1)
    bcast = lambda g: jnp.broadcast_to(g[:, None], (g.shape[0], LANE))
    t_bf = lambda m: m.T.astype(BF16)
    return {
        "w1qt": t_bf(col(0, 1)), "w1kvt": t_bf(col(1, 2)), "w1n": w1n.astype(BF16),
        "qng": bcast(mla_q_norm[l]), "kvng": bcast(mla_kv_norm[l]), "kvn": mla_kv_norm[l].reshape(1, -1),
        "wuqt": t_bf(wuq.reshape(MLA_Q_RANK, MLA_HEADS * LANE) * mla_scale),
        "wuk": wuk.reshape(MLA_KV_RANK, -1).astype(BF16),
        "wuvt": t_bf(wuv.reshape(MLA_KV_RANK, -1)),
        "wdqt": t_bf(col(3, 4) * hd_scale), "wdk": col(4, 5).astype(BF16), "wdvt": t_bf(col(5, 6)),
        "wsqt": t_bf(col(6, 7) * hd_scale), "wsk": col(7, 8).astype(BF16), "wsvt": t_bf(col(8, 9)),
        "wz": col(9, 10).astype(BF16),
        "wg": col(10, 11).astype(BF16),
        "wb": w_branch[l].astype(BF16),
        "wo": w_out[l].astype(BF16),
    }


def _rope_tables(n_lat, ctx_len):
    rows = n_lat // GRID_W
    row = jnp.repeat(jnp.arange(rows), GRID_W).astype(F32)
    col = jnp.tile(jnp.arange(GRID_W), rows).astype(F32)

    def cs(rot_dim):
        n_freq = rot_dim // 4
        inv_freq = ROPE_BASE ** (-jnp.arange(n_freq, dtype=F32) / n_freq)
        ang = jnp.concatenate([row[:, None] * inv_freq, col[:, None] * inv_freq], axis=-1)
        cos = jnp.concatenate([jnp.cos(ang), jnp.ones((ctx_len, rot_dim // 2), F32)], axis=0)
        sin = jnp.concatenate([jnp.sin(ang), jnp.zeros((ctx_len, rot_dim // 2), F32)], axis=0)
        return cos, sin

    cos, sin = cs(SWA_HD)
    cch = jnp.tile(cos, (1, LANE // (SWA_HD // 2)))
    ssh = jnp.tile(jnp.concatenate([-sin, sin], axis=1), (1, LANE // SWA_HD))
    cos, sin = cs(MLA_ROPE)
    t = cos.shape[0]
    pad = LANE - MLA_NOPE - MLA_ROPE
    ccm = jnp.concatenate([jnp.ones((t, MLA_NOPE), F32), cos, cos, jnp.zeros((t, pad), F32)], axis=1)
    ssm = jnp.concatenate([jnp.zeros((t, MLA_NOPE), F32), -sin, sin, jnp.zeros((t, pad), F32)], axis=1)
    return {"cch": cch, "ssh": ssh, "ccm": ccm, "ssm": ssm,
            "ccht": cch.T, "ssht": ssh.T, "ccmt": ccm.T, "ssmt": ssm.T}


def kernel(x, c, ctx, c_ctx, w_mod, b_mod, norm_g, w_in, mla_q_norm, mla_w_uq, mla_kv_norm, mla_w_ukv,
           diff_lq1, diff_lk1, diff_lq2, diff_lk2, diff_subln, swa_sink, w_branch, w_out, final_norm):
    b, n_lat, d = x.shape
    ctx_len = ctx.shape[1]
    depth = w_mod.shape[0]
    assert ctx_len == ROW_TM and n_lat % ROW_TM == 0 and n_lat % GRID_W == 0
    assert n_lat >= ROW_TM + 2 * WINDOW and (n_lat + ctx_len) % MXU_DIM == 0 and b < 16

    x_all = jnp.concatenate([x, ctx], axis=1)
    tabs = _rope_tables(n_lat, ctx_len)

    c_rows = jnp.concatenate([c, c_ctx[None, :], jnp.zeros((16 - b - 1, d), c.dtype)], axis=0)
    mod = _modulation(c_rows, w_mod, b_mod)

    out = None
    for l in range(depth):
        need_ctx = l < depth - 1
        mod_lat = mod[l, :b].reshape(b, 1, 3, d)
        mod_ctx = jnp.broadcast_to(mod[l, b].reshape(1, 1, 3, d), (b, 1, 3, d))
        mod4 = jnp.concatenate([mod_ctx, mod_lat], axis=1)
        wts = _layer_weights(l, w_in, mla_q_norm, mla_w_uq, mla_kv_norm, mla_w_ukv, w_branch, w_out)
        ng = norm_g[l].reshape(1, d)

        mqt, mk, mvt, dqt, dk, dvt, sqt, sk, svt = _project(x_all, mod4, ng, wts, tabs, n_lat)

        lam_init = 0.8 - 0.6 * math.exp(-0.3 * l)
        lam = (jnp.exp(jnp.sum(diff_lq1[l].astype(F32) * diff_lk1[l].astype(F32)))
               - jnp.exp(jnp.sum(diff_lq2[l].astype(F32) * diff_lk2[l].astype(F32))) + lam_init).reshape(1)

        ya = _flash(mqt, mk, mvt, n_sub=1, n_lat=n_lat, need_ctx=need_ctx)
        yb = _flash(dqt, dk, dvt, n_sub=2, n_lat=n_lat, need_ctx=need_ctx,
                    lam=lam, subln=diff_subln[l].reshape(LANE, 1), lam_scale=1.0 - lam_init)
        yc = _swa(swa_sink[l], sqt, sk, svt, n_lat=n_lat, need_ctx=need_ctx)

        if need_ctx:
            x_all = _merge(x_all, mod4, ng, ya, yb, yc, wts, n_lat=n_lat)
        else:
            out = _merge(x_all, mod4, ng, ya, yb, yc, wts, n_lat=n_lat, final_norm=final_norm.reshape(1, d))
    return out
```
